```python
import jax, jax.numpy as jnp
from jax import lax
import numpy as np

D_MODEL = 4096
BATCH = 16
SEQ = 256
DEPTH = 4
DEC_BATCH = 4
DEC_SEQ = 4096
PAST_LEN = 256

GRID_W = 64
N_HEADS = 16
N_KV = 4
HEAD_DIM = 64
ATT_W = N_HEADS * HEAD_DIM
KV_W = N_KV * HEAD_DIM
WINDOW = 128
BLOCK = 128
ROPE_THETA = 10000.0
SSD_W = 1024
SSD_HEAD_DIM = 64
SSD_HEADS = SSD_W // SSD_HEAD_DIM
SSD_GROUPS = 2
SSD_STATE = 128
SSD_CONV = 5
SSD_CHUNK = 128
SSD_CONV_CH = SSD_W + 2 * SSD_GROUPS * SSD_STATE
HGRN_W = 1024
HGRN_DK = 128
HGRN_DV = 128
HGRN_HEADS = HGRN_W // HGRN_DK
HGRN_CHUNK = 16
GATE_RANK = 512
D_FF = 5632
N_EXPERTS = 8
TOP_K = 2
EXPERT_FF = 1408
MOE_BLOCK = 1024
N_DENSE = (DEPTH + 1) // 2
N_MOE = DEPTH // 2
EPS = 1e-6
N_IN = ATT_W + 2 * KV_W + SSD_W + SSD_CONV_CH + 2 * SSD_HEADS + 5 * HGRN_W + GATE_RANK

kernel_name = 'hybrid_diffusion_ssd_hgrn2_swa_step'


def rms_norm(x, g):
    xf = x.astype(jnp.float32)
    y = xf * lax.rsqrt(jnp.mean(xf * xf, axis=-1, keepdims=True) + EPS)
    return (y * g.astype(jnp.float32)).astype(x.dtype)


def flip(t):
    return jnp.flip(t, axis=1)


def swiglu(x, wg, wu, wd):
    return (jax.nn.silu(x @ wg) * (x @ wu)) @ wd


def moe_swiglu(x, w_router, wg, wu, wd):
    shp = x.shape
    D = shp[-1]
    t = x.reshape(-1, D)
    N = t.shape[0]
    NK = N * TOP_K
    logits = (t @ w_router).astype(jnp.float32)
    top_v, top_i = lax.top_k(logits, TOP_K)
    top_w = jax.nn.softmax(top_v, axis=-1)
    e_flat = top_i.reshape(-1).astype(jnp.int32)
    order = jnp.argsort(e_flat)
    e_sorted = e_flat[order]
    counts = jnp.bincount(e_flat, length=N_EXPERTS).astype(jnp.int32)
    padded = (counts + MOE_BLOCK - 1) // MOE_BLOCK * MOE_BLOCK
    start = jnp.cumsum(counts) - counts
    pend = jnp.cumsum(padded)
    pstart = pend - padded
    pos_sorted = (pstart[e_sorted] + jnp.arange(NK, dtype=jnp.int32) - start[e_sorted]).astype(jnp.int32)
    n_blocks = -(-NK // MOE_BLOCK) + N_EXPERTS
    P = n_blocks * MOE_BLOCK
    slot_tok = jnp.full((P,), N, jnp.int32).at[pos_sorted].set((order // TOP_K).astype(jnp.int32))
    t_pad = jnp.concatenate([t, jnp.zeros((1, D), t.dtype)], axis=0)
    blk_e = jnp.minimum(jnp.searchsorted(pend, jnp.arange(n_blocks, dtype=jnp.int32) * MOE_BLOCK, side='right'),
                        N_EXPERTS - 1)

    def one_block(args):
        idx, e = args
        return swiglu(t_pad[idx], wg[e], wu[e], wd[e])

    out = lax.map(one_block, (slot_tok.reshape(n_blocks, MOE_BLOCK), blk_e)).reshape(P, D)
    pos = jnp.zeros((NK,), jnp.int32).at[order].set(pos_sorted)
    y = jnp.einsum('nk,nkd->nd', top_w.astype(x.dtype), out[pos].reshape(N, TOP_K, D))
    return y.reshape(shp)


def axial_rope(x):
    T = x.shape[1]
    rows = T // GRID_W
    row = jnp.repeat(jnp.arange(rows), GRID_W)
    col = jnp.tile(jnp.arange(GRID_W), rows)
    half = HEAD_DIM // 2
    quarter = HEAD_DIM // 4
    inv = ROPE_THETA ** (-jnp.arange(quarter, dtype=jnp.float32) / quarter)

    def rot(xh, pos):
        ang = pos.astype(jnp.float32)[:, None] * inv[None, :]
        cos = jnp.cos(ang)[None, :, None, :].astype(x.dtype)
        sin = jnp.sin(ang)[None, :, None, :].astype(x.dtype)
        x1, x2 = xh[..., :quarter], xh[..., quarter:]
        return jnp.concatenate([x1 * cos - x2 * sin, x2 * cos + x1 * sin], axis=-1)

    return jnp.concatenate([rot(x[..., :half], row), rot(x[..., half:], col)], axis=-1)


def sink_softmax(s, sink):
    g = N_HEADS // N_KV
    sk = jnp.broadcast_to(sink.astype(jnp.float32).reshape(N_KV, g)[None, :, :, None, None], s.shape[:-1] + (1,))
    return jax.nn.softmax(jnp.concatenate([sk, s], axis=-1), axis=-1)[..., 1:]


def context_attention(q, k, v, sink):
    b, C = q.shape[:2]
    g = N_HEADS // N_KV
    nb = C // BLOCK
    scale = HEAD_DIM ** -0.5
    qb = jnp.moveaxis(q.reshape(b, nb, BLOCK, N_KV, g, HEAD_DIM), 1, 0)

    def one_block(qblk):
        s = jnp.einsum('bqkgd,bckd->bkgqc', qblk, k, preferred_element_type=jnp.float32) * scale
        p = sink_softmax(s, sink)
        o = jnp.einsum('bkgqc,bckd->bqkgd', p.astype(v.dtype), v)
        return o.reshape(b, BLOCK, ATT_W)

    out = lax.map(one_block, qb)
    return jnp.moveaxis(out, 0, 1).reshape(b, C, ATT_W)


def latent_attention(q, k, v, k_ctx, v_ctx, sink):
    b, T = q.shape[:2]
    g = N_HEADS // N_KV
    nb = T // BLOCK
    C = k_ctx.shape[1]
    scale = HEAD_DIM ** -0.5
    qb = jnp.moveaxis(q.reshape(b, nb, BLOCK, N_KV, g, HEAD_DIM), 1, 0)
    k_pad = jnp.pad(k, ((0, 0), (BLOCK, BLOCK), (0, 0), (0, 0)))
    v_pad = jnp.pad(v, ((0, 0), (BLOCK, BLOCK), (0, 0), (0, 0)))

    def one_block(args):
        n, qblk = args
        start = n * BLOCK
        kw = lax.dynamic_slice_in_dim(k_pad, start, 3 * BLOCK, axis=1)
        vw = lax.dynamic_slice_in_dim(v_pad, start, 3 * BLOCK, axis=1)
        qpos = start + jnp.arange(BLOCK)
        kpos = start - BLOCK + jnp.arange(3 * BLOCK)
        valid = (jnp.abs(qpos[:, None] - kpos[None, :]) <= WINDOW) & (kpos[None, :] >= 0) & (kpos[None, :] < T)
        s_loc = jnp.einsum('bqkgd,bskd->bkgqs', qblk, kw, preferred_element_type=jnp.float32) * scale
        s_loc = jnp.where(valid, s_loc, -jnp.inf)
        s_ctx = jnp.einsum('bqkgd,bckd->bkgqc', qblk, k_ctx, preferred_element_type=jnp.float32) * scale
        p = sink_softmax(jnp.concatenate([s_ctx, s_loc], axis=-1), sink).astype(v.dtype)
        o = (jnp.einsum('bkgqc,bckd->bqkgd', p[..., :C], v_ctx.astype(v.dtype))
             + jnp.einsum('bkgqs,bskd->bqkgd', p[..., C:], vw))
        return o.reshape(b, BLOCK, ATT_W)

    out = lax.map(one_block, (jnp.arange(nb), qb))
    return jnp.moveaxis(out, 0, 1).reshape(b, T, ATT_W)


def depthwise_conv(x, w, bias):
    pad = SSD_CONV // 2
    y = lax.conv_general_dilated(x, w[:, None, :].astype(x.dtype), window_strides=(1,), padding=[(pad, pad)],
                                 dimension_numbers=('NWC', 'WIO', 'NWC'), feature_group_count=x.shape[-1])
    return y + bias.astype(x.dtype)


def ssd_scan(x, dt, a, bm, cm, h0):
    f32 = jnp.float32
    b, T = x.shape[:2]
    L = SSD_CHUNK
    nc = T // L
    hg = SSD_HEADS // SSD_GROUPS
    la = (dt * a).reshape(b, nc, L, SSD_GROUPS, hg)
    xdt = (x.astype(f32) * dt[..., None]).reshape(b, nc, L, SSD_GROUPS, hg, SSD_HEAD_DIM)
    bm = bm.astype(f32).reshape(b, nc, L, SSD_GROUPS, SSD_STATE)
    cm = cm.astype(f32).reshape(b, nc, L, SSD_GROUPS, SSD_STATE)
    acum = jnp.cumsum(la, axis=2)
    causal = jnp.tril(jnp.ones((L, L), bool))[None, None, :, :, None, None]
    seg = acum[:, :, :, None] - acum[:, :, None, :]
    decay = jnp.exp(jnp.where(causal, seg, -jnp.inf))
    cb = jnp.einsum('bcign,bcjgn->bcijg', cm, bm)
    y_diag = jnp.einsum('bcijgh,bcjghp->bcighp', cb[..., None] * decay, xdt)
    to_end = jnp.exp(acum[:, :, -1:] - acum)
    states = jnp.einsum('bcjgn,bcjghp->bcghpn', bm, xdt * to_end[..., None])
    chunk_decay = jnp.exp(acum[:, :, -1])

    def step(h, inp):
        st, dec = inp
        return dec[..., None, None] * h + st, h

    h0 = h0.astype(f32).reshape(b, SSD_GROUPS, hg, SSD_HEAD_DIM, SSD_STATE)
    h_fin, h_in = lax.scan(step, h0, (jnp.moveaxis(states, 1, 0), jnp.moveaxis(chunk_decay, 1, 0)))
    h_in = jnp.moveaxis(h_in, 0, 1)
    y_off = jnp.einsum('bcign,bcghpn->bcighp', cm, h_in) * jnp.exp(acum)[..., None]
    y = (y_diag + y_off).reshape(b, T, SSD_HEADS, SSD_HEAD_DIM)
    return y, h_fin.reshape(b, SSD_HEADS, SSD_HEAD_DIM, SSD_STATE)


def forget_gate(raw, lb):
    r = raw.astype(jnp.float32).reshape(raw.shape[0], raw.shape[1], HGRN_HEADS, HGRN_DK)
    logf = jnp.log(lb + (1.0 - lb) * jax.nn.sigmoid(r))
    k = (1.0 - lb) * jax.nn.sigmoid(-r)
    return logf, k


def hgrn_scan(q, k, v, logf, s0):
    f32 = jnp.float32
    b, T = q.shape[:2]
    L = HGRN_CHUNK
    nc = T // L

    def chunks(t):
        return jnp.moveaxis(t.astype(f32).reshape((b, nc, L) + t.shape[2:]), 1, 0)

    mask = jnp.tril(jnp.ones((L, L), bool))[None, :, :, None, None]

    def step(S, inp):
        qc, kc, vc, fc = inp
        bc = jnp.cumsum(fc, axis=1)
        dec = jnp.exp(jnp.where(mask, bc[:, :, None] - bc[:, None, :], -jnp.inf))
        att = jnp.einsum('bijhd,bjhd->bhij', dec * qc[:, :, None], kc)
        o = jnp.einsum('bhij,bjhv->bihv', att, vc) + jnp.einsum('bihd,bhdv->bihv', qc * jnp.exp(bc), S)
        S_new = jnp.exp(bc[:, -1])[..., None] * S + jnp.einsum('bjhd,bjhv->bhdv', kc * jnp.exp(bc[:, -1:] - bc), vc)
        return S_new, o

    S_fin, o = lax.scan(step, s0.astype(f32), (chunks(q), chunks(k), chunks(v), chunks(logf)))
    return jnp.moveaxis(o, 0, 1).reshape(b, T, HGRN_HEADS, HGRN_DV), S_fin


def mixer(h, lw, cached):
    f32 = jnp.float32
    b, T, _ = h.shape
    sizes = (ATT_W, KV_W, KV_W, SSD_W, SSD_CONV_CH, 2 * SSD_HEADS,
             HGRN_W, HGRN_W, HGRN_W, HGRN_W, HGRN_W, GATE_RANK)
    cuts = [int(c) for c in np.cumsum(sizes)[:-1]]
    (q, k, v, z, xbc, dt_raw, hq, hf_f, hf_b, hi, hg, g_code) = jnp.split(h @ lw['w_in'], cuts, axis=-1)
    q = q.reshape(b, T, N_HEADS, HEAD_DIM)
    k = k.reshape(b, T, N_KV, HEAD_DIM)
    v = v.reshape(b, T, N_KV, HEAD_DIM)
    if cached is None:
        y_att = context_attention(q, k, v, lw['sink'])
        ssd0 = jnp.zeros((b, 2, SSD_HEADS, SSD_HEAD_DIM, SSD_STATE), f32)
        hgrn0 = jnp.zeros((b, 2, HGRN_HEADS, HGRN_DK, HGRN_DV), f32)
    else:
        k_ctx, v_ctx, ssd0, hgrn0 = cached
        y_att = latent_attention(axial_rope(q), axial_rope(k), v, k_ctx.astype(k.dtype), v_ctx, lw['sink'])
    xbc = jax.nn.silu(depthwise_conv(xbc, lw['conv_w'], lw['conv_b']))
    xs, bm, cm = jnp.split(xbc, [SSD_W, SSD_W + SSD_GROUPS * SSD_STATE], axis=-1)
    xs = xs.reshape(b, T, SSD_HEADS, SSD_HEAD_DIM)
    bm = bm.reshape(b, T, SSD_GROUPS, SSD_STATE)
    cm = cm.reshape(b, T, SSD_GROUPS, SSD_STATE)
    dt = jax.nn.softplus(dt_raw.astype(f32).reshape(b, T, 2, SSD_HEADS) + lw['dt_bias'].astype(f32))
    a = -jnp.exp(lw['a_log'].astype(f32))
    y_f, hs_f = ssd_scan(xs, dt[:, :, 0], a[0], bm, cm, ssd0[:, 0])
    y_b, hs_b = ssd_scan(flip(xs), flip(dt[:, :, 1]), a[1], flip(bm), flip(cm), ssd0[:, 1])
    y_ssd = y_f + flip(y_b) + lw['ssd_d'].astype(f32)[:, None] * xs.astype(f32)
    y_ssd = rms_norm(y_ssd.reshape(b, T, SSD_W) * jax.nn.silu(z.astype(f32)), lw['ssd_norm']).astype(h.dtype)
    hq = jax.nn.silu(hq).reshape(b, T, HGRN_HEADS, HGRN_DK)
    hi = hi.reshape(b, T, HGRN_HEADS, HGRN_DV)
    lb = lw['lb'].reshape(2, HGRN_HEADS, HGRN_DK)
    logf_f, k_f = forget_gate(hf_f, lb[0])
    logf_b, k_b = forget_gate(hf_b, lb[1])
    o_f, s_f = hgrn_scan(hq, k_f, hi, logf_f, hgrn0[:, 0])
    o_b, s_b = hgrn_scan(flip(hq), flip(k_b), flip(hi), flip(logf_b), hgrn0[:, 1])
    y_hgrn = rms_norm((o_f + flip(o_b)).reshape(b, T, HGRN_W), lw['hgrn_norm']) * jax.nn.silu(hg.astype(f32))
    y_hgrn = y_hgrn.astype(h.dtype)
    g_att, g_ssd, g_hgrn = jnp.split(jax.nn.sigmoid(g_code @ lw['w_gate']), 3, axis=-1)
    merged = (g_att * (y_att @ lw['w_br_att'])
              + g_ssd * (y_ssd @ lw['w_br_ssd'])
              + g_hgrn * (y_hgrn @ lw['w_br_hgrn']))
    out = merged @ lw['w_out']
    if cached is None:
        return out, (k, v, jnp.stack([hs_f, hs_b], axis=1), jnp.stack([s_f, s_b], axis=1))
    return out, None


def block(x, m, lw, ffn, cached):
    mod = jax.nn.silu(m) @ lw['w_ada'] + lw['b_ada']
    sh1, sc1, g1, sh2, sc2, g2 = [t[:, None, :] for t in jnp.split(mod, 6, axis=-1)]
    h = rms_norm(x, lw['norm1']) * (1 + sc1) + sh1
    mix, ctx_t = mixer(h, lw, cached)
    x = x + g1 * mix
    h2 = rms_norm(x, lw['norm2']) * (1 + sc2) + sh2
    x = x + g2 * ffn(h2)
    return x, ctx_t


def setup_inputs(seed: int = 0) -> dict:
    key = jax.random.key(seed)
    kit = iter(list(jax.random.split(key, 40)))
    f32 = jnp.float32
    D = D_MODEL

    def nrm(shape, s):
        return jax.random.normal(next(kit), shape, f32) * s

    dt0 = jnp.exp(jax.random.uniform(next(kit), (DEPTH, 2, SSD_HEADS), f32, np.log(1e-3), np.log(1e-1)))
    return {
        'x_prompt': nrm((BATCH, SEQ, D), 1.0),
        'x_sample': nrm((DEC_BATCH, DEC_SEQ, D), 1.0),
        'cache_k': nrm((DEC_BATCH, DEPTH, PAST_LEN, N_KV, HEAD_DIM), 1.0),
        'cache_v': nrm((DEC_BATCH, DEPTH, PAST_LEN, N_KV, HEAD_DIM), 1.0),
        'state_ssd': nrm((DEC_BATCH, DEPTH, 2, SSD_HEADS, SSD_HEAD_DIM, SSD_STATE), 0.3),
        'state_hgrn': nrm((DEC_BATCH, DEPTH, 2, HGRN_HEADS, HGRN_DK, HGRN_DV), 0.5),
        'c': nrm((DEC_BATCH, D), 1.0),
        'c_ctx': nrm((D,), 1.0),
        'w_ada': nrm((DEPTH, D, 6 * D), 0.5 * D ** -0.5),
        'b_ada': nrm((DEPTH, 6 * D), 0.02),
        'norm1_g': 1.0 + nrm((DEPTH, D), 0.02),
        'norm2_g': 1.0 + nrm((DEPTH, D), 0.02),
        'w_in': nrm((DEPTH, D, N_IN), D ** -0.5),
        'attn_sink': nrm((DEPTH, N_HEADS), 0.5),
        'ssd_conv_w': nrm((DEPTH, SSD_CONV, SSD_CONV_CH), SSD_CONV ** -0.5),
        'ssd_conv_b': nrm((DEPTH, SSD_CONV_CH), 0.02),
        'ssd_dt_bias': dt0 + jnp.log(-jnp.expm1(-dt0)),
        'ssd_a_log': jnp.log(jax.random.uniform(next(kit), (DEPTH, 2, SSD_HEADS), f32, 1.0, 16.0)),
        'ssd_d': 1.0 + nrm((DEPTH, SSD_HEADS), 0.02),
        'ssd_norm_g': 1.0 + nrm((DEPTH, SSD_W), 0.02),
        'hgrn_lb': nrm((2, DEPTH, HGRN_W), 0.5),
        'hgrn_norm_g': 1.0 + nrm((DEPTH, HGRN_W), 0.02),
        'w_gate': nrm((DEPTH, GATE_RANK, 3 * D), GATE_RANK ** -0.5),
        'w_br_att': nrm((DEPTH, ATT_W, D), ATT_W ** -0.5),
        'w_br_ssd': nrm((DEPTH, SSD_W, D), SSD_W ** -0.5),
        'w_br_hgrn': nrm((DEPTH, HGRN_W, D), HGRN_W ** -0.5),
        'w_out': nrm((DEPTH, D, D), D ** -0.5),
        'ffn_wg': nrm((N_DENSE, D, D_FF), D ** -0.5),
        'ffn_wu': nrm((N_DENSE, D, D_FF), D ** -0.5),
        'ffn_wd': nrm((N_DENSE, D_FF, D), D_FF ** -0.5),
        'router_w': nrm((N_MOE, D, N_EXPERTS), D ** -0.5),
        'moe_wg': nrm((N_MOE, N_EXPERTS, D, EXPERT_FF), D ** -0.5),
        'moe_wu': nrm((N_MOE, N_EXPERTS, D, EXPERT_FF), D ** -0.5),
        'moe_wd': nrm((N_MOE, N_EXPERTS, EXPERT_FF, D), EXPERT_FF ** -0.5),
        'final_g': 1.0 + nrm((D,), 0.02),
    }


def reference(x_prompt, x_sample, cache_k, cache_v, state_ssd, state_hgrn, c, c_ctx,
              w_ada, b_ada, norm1_g, norm2_g, w_in, attn_sink, ssd_conv_w, ssd_conv_b,
              ssd_dt_bias, ssd_a_log, ssd_d, ssd_norm_g, hgrn_lb, hgrn_norm_g, w_gate,
              w_br_att, w_br_ssd, w_br_hgrn, w_out, ffn_wg, ffn_wu, ffn_wd,
              router_w, moe_wg, moe_wu, moe_wd, final_g):
    sm = jax.nn.softmax(hgrn_lb.astype(jnp.float32), axis=1)
    lb_all = jnp.cumsum(sm, axis=1) - sm[:, :1]

    def layer_weights(l):
        return {'w_ada': w_ada[l], 'b_ada': b_ada[l], 'norm1': norm1_g[l], 'norm2': norm2_g[l],
                'w_in': w_in[l], 'sink': attn_sink[l], 'conv_w': ssd_conv_w[l], 'conv_b': ssd_conv_b[l],
                'dt_bias': ssd_dt_bias[l], 'a_log': ssd_a_log[l], 'ssd_d': ssd_d[l], 'ssd_norm': ssd_norm_g[l],
                'lb': lb_all[:, l], 'hgrn_norm': hgrn_norm_g[l], 'w_gate': w_gate[l], 'w_br_att': w_br_att[l],
                'w_br_ssd': w_br_ssd[l], 'w_br_hgrn': w_br_hgrn[l], 'w_out': w_out[l]}

    def channel_mixer(t, l):
        i = l // 2
        if l % 2 == 0:
            return swiglu(t, ffn_wg[i], ffn_wu[i], ffn_wd[i])
        return moe_swiglu(t, router_w[i], moe_wg[i], moe_wu[i], moe_wd[i])

    xp = x_prompt
    ks, vs, hs, ss = [], [], [], []
    for l in range(DEPTH):
        xp, ctx_t = block(xp, c_ctx[None, :], layer_weights(l), lambda t, l=l: channel_mixer(t, l), None)
        ks.append(ctx_t[0])
        vs.append(ctx_t[1])
        hs.append(ctx_t[2])
        ss.append(ctx_t[3])
    y_prompt = rms_norm(xp, final_g)
    new_cache_k = jnp.stack(ks, axis=1)
    new_cache_v = jnp.stack(vs, axis=1)
    new_state_ssd = jnp.stack(hs, axis=1)
    new_state_hgrn = jnp.stack(ss, axis=1)

    xs = x_sample
    for l in range(DEPTH):
        cached = (cache_k[:, l], cache_v[:, l], state_ssd[:, l], state_hgrn[:, l])
        xs, _ = block(xs, c, layer_weights(l), lambda t, l=l: channel_mixer(t, l), cached)
    y_sample = rms_norm(xs, final_g)

    return (y_prompt, y_sample, new_cache_k, new_cache_v, new_state_ssd, new_state_hgrn)
```

```python
import functools

import numpy as np
import jax
import jax.numpy as jnp
from jax import lax
from jax.experimental import pallas as pl
from jax.experimental.pallas import tpu as pltpu

f32 = jnp.float32
bf16 = jnp.bfloat16

D_MODEL = 4096
BATCH = 16
SEQ = 256
DEPTH = 4
DEC_BATCH = 4
DEC_SEQ = 4096
PAST_LEN = 256
GRID_W = 64
N_HEADS = 16
N_KV = 4
HEAD_DIM = 64
WINDOW = 128
BLOCK = 128
ROPE_THETA = 10000.0
SSD_W = 1024
SSD_HEAD_DIM = 64
SSD_GROUPS = 2
SSD_STATE = 128
SSD_CONV = 5
SSD_CHUNK = 128
HGRN_W = 1024
HGRN_DK = 128
HGRN_DV = 128
HGRN_CHUNK = 16
GATE_RANK = 512
D_FF = 5632
N_EXPERTS = 8
TOP_K = 2
EXPERT_FF = 1408
EPS = 1e-6

LANE = 128
SUBLANE = 8
VMEM_LIMIT = 56 * 1024 * 1024
MOE_TILE = 512
HGRN_ROWS = 128
HIGHEST = lax.Precision.HIGHEST


def _cparams(*sem):
    return pltpu.CompilerParams(dimension_semantics=sem, vmem_limit_bytes=VMEM_LIMIT)


def _pick(n, prefs):
    for p in prefs:
        if n % p == 0:
            return p
    return n


def _silu(x):
    return x * (1.0 / (1.0 + jnp.exp(-x)))


def _sigmoid(x):
    return 1.0 / (1.0 + jnp.exp(-x))


def _dims():
    att_w = N_HEADS * HEAD_DIM
    kv_w = N_KV * HEAD_DIM
    ssd_heads = SSD_W // SSD_HEAD_DIM
    conv_ch = SSD_W + 2 * SSD_GROUPS * SSD_STATE
    hgrn_heads = HGRN_W // HGRN_DK
    nc = BATCH * SEQ
    nl = DEC_BATCH * DEC_SEQ
    return att_w, kv_w, ssd_heads, conv_ch, hgrn_heads, nc, nl


def _group_of_tile(i, tm):
    nc = BATCH * SEQ
    r = i * tm
    return jnp.where(r < nc, 0, 1 + (r - nc) // DEC_SEQ)


def _row_tile(prefs=(512, 256, 128)):
    nc = BATCH * SEQ
    return _pick(int(np.gcd(nc, DEC_SEQ)), prefs)


def _ada_kernel(m_ref, w_ref, b_ref, o_ref):
    a = _silu(m_ref[...]).astype(bf16)
    o_ref[...] = jnp.dot(a, w_ref[...].astype(bf16), preferred_element_type=f32) + b_ref[...]


def ada_mod(m_pad, w_ada, b_ada):
    L, D, N = w_ada.shape
    G = m_pad.shape[0]
    tn = _pick(N, (512, 256, 128))
    return pl.pallas_call(
        _ada_kernel,
        grid=(L, N // tn),
        in_specs=[pl.BlockSpec((G, D), lambda l, j: (0, 0)),
                  pl.BlockSpec((None, D, tn), lambda l, j: (l, 0, j)),
                  pl.BlockSpec((None, 1, tn), lambda l, j: (l, 0, j))],
        out_specs=pl.BlockSpec((None, G, tn), lambda l, j: (l, 0, j)),
        out_shape=jax.ShapeDtypeStruct((L, G, N), f32),
        compiler_params=_cparams("parallel", "parallel"),
        name="ada_mod",
    )(m_pad, w_ada, b_ada.reshape(L, 1, N))


def _norm_kernel(modulated, x_ref, g_ref, *rest):
    if modulated:
        sc_ref, sh_ref, o_ref = rest
    else:
        (o_ref,) = rest
    x = x_ref[...]
    y = x * lax.rsqrt(jnp.mean(x * x, axis=-1, keepdims=True) + EPS) * g_ref[...]
    if modulated:
        y = y * (1.0 + sc_ref[...]) + sh_ref[...]
    o_ref[...] = y.astype(o_ref.dtype)


def norm_mod(x, g, sc=None, sh=None, out_dtype=bf16):
    M, D = x.shape
    tm = _row_tile((256, 128))
    modulated = sc is not None
    row = pl.BlockSpec((tm, D), lambda i: (i, 0))
    in_specs = [row, pl.BlockSpec((1, D), lambda i: (0, 0))]
    args = [x, g.reshape(1, D)]
    if modulated:
        grp = pl.BlockSpec((None, 1, D), lambda i: (_group_of_tile(i, tm), 0, 0))
        in_specs += [grp, grp]
        args += [sc, sh]
    return pl.pallas_call(
        functools.partial(_norm_kernel, modulated),
        grid=(M // tm,),
        in_specs=in_specs,
        out_specs=row,
        out_shape=jax.ShapeDtypeStruct((M, D), out_dtype),
        compiler_params=_cparams("parallel"),
        name="norm_mod",
    )(*args)


def _router_kernel(h_ref, w_ref, o_ref):
    o_ref[...] = jnp.dot(h_ref[...], w_ref[...], preferred_element_type=f32, precision=HIGHEST)


def router_logits(h2, w_router_pad):
    M, D = h2.shape
    tm = _row_tile((256, 128))
    return pl.pallas_call(
        _router_kernel,
        grid=(M // tm,),
        in_specs=[pl.BlockSpec((tm, D), lambda i: (i, 0)), pl.BlockSpec((D, LANE), lambda i: (0, 0))],
        out_specs=pl.BlockSpec((tm, LANE), lambda i: (i, 0)),
        out_shape=jax.ShapeDtypeStruct((M, LANE), f32),
        compiler_params=_cparams("parallel"),
        name="router_logits",
    )(h2, w_router_pad)


def _mm_kernel(x_ref, w_ref, o_ref):
    o_ref[...] = jnp.dot(x_ref[...].astype(bf16), w_ref[...], preferred_element_type=f32).astype(o_ref.dtype)


def matmul(x, w, out_dtype, name="matmul"):
    M, K = x.shape
    N = w.shape[1]
    tm = _row_tile((1024, 512, 256, 128))
    tn = _pick(N, (1024, 896, 768, 512, 256, 128))
    return pl.pallas_call(
        _mm_kernel,
        grid=(N // tn, M // tm),
        in_specs=[pl.BlockSpec((tm, K), lambda j, i: (i, 0)), pl.BlockSpec((K, tn), lambda j, i: (0, j))],
        out_specs=pl.BlockSpec((tm, tn), lambda j, i: (i, j)),
        out_shape=jax.ShapeDtypeStruct((M, N), out_dtype),
        compiler_params=_cparams("parallel", "parallel"),
        name=name,
    )(x, w)


def _mm_res_kernel(a_ref, w_ref, x_ref, g_ref, o_ref):
    y = jnp.dot(a_ref[...], w_ref[...], preferred_element_type=f32)
    o_ref[...] = x_ref[...] + g_ref[...] * y


def matmul_residual(a, w, x, gate, name="matmul_residual"):
    M, K = a.shape
    N = w.shape[1]
    tm = _row_tile((512, 256, 128))
    tn = _pick(N, (1024, 512, 256, 128))
    return pl.pallas_call(
        _mm_res_kernel,
        grid=(N // tn, M // tm),
        in_specs=[pl.BlockSpec((tm, K), lambda j, i: (i, 0)),
                  pl.BlockSpec((K, tn), lambda j, i: (0, j)),
                  pl.BlockSpec((tm, tn), lambda j, i: (i, j)),
                  pl.BlockSpec((None, 1, tn), lambda j, i: (_group_of_tile(i, tm), 0, j))],
        out_specs=pl.BlockSpec((tm, tn), lambda j, i: (i, j)),
        out_shape=jax.ShapeDtypeStruct((M, N), f32),
        compiler_params=_cparams("parallel", "parallel"),
        name=name,
    )(a, w, x, gate)


def _glu_kernel(h_ref, wg_ref, wu_ref, o_ref):
    h = h_ref[...]
    a = jnp.dot(h, wg_ref[...], preferred_element_type=f32)
    u = jnp.dot(h, wu_ref[...], preferred_element_type=f32)
    o_ref[...] = (_silu(a) * u).astype(o_ref.dtype)


def glu_up(h, wg, wu):
    M, K = h.shape
    N = wg.shape[1]
    tm = _row_tile((512, 256, 128))
    tn = _pick(N, (512, 256, 128))
    wspec = pl.BlockSpec((K, tn), lambda j, i: (0, j))
    return pl.pallas_call(
        _glu_kernel,
        grid=(N // tn, M // tm),
        in_specs=[pl.BlockSpec((tm, K), lambda j, i: (i, 0)), wspec, wspec],
        out_specs=pl.BlockSpec((tm, tn), lambda j, i: (i, j)),
        out_shape=jax.ShapeDtypeStruct((M, N), bf16),
        compiler_params=_cparams("parallel", "parallel"),
        name="glu_up",
    )(h, wg, wu)


def _merge_kernel(gc_ref, ya_ref, ys_ref, yh_ref, wga_ref, wgs_ref, wgh_ref, wa_ref, ws_ref, wh_ref, o_ref):
    gc = gc_ref[...]

    def branch(y_ref, wg_ref, w_ref):
        gate = _sigmoid(jnp.dot(gc, wg_ref[...], preferred_element_type=f32))
        return gate * jnp.dot(y_ref[...], w_ref[...], preferred_element_type=f32)

    o_ref[...] = (branch(ya_ref, wga_ref, wa_ref) + branch(ys_ref, wgs_ref, ws_ref)
                  + branch(yh_ref, wgh_ref, wh_ref)).astype(o_ref.dtype)


def merge_branches(gc, y_att, y_ssd, y_hgrn, w_gate, w_att, w_ssd, w_hgrn):
    M, R = gc.shape
    D = w_att.shape[1]
    tm = _row_tile((512, 256, 128))
    tn = _pick(D, (1024, 512, 256, 128))
    nj = D // tn

    def rows(w):
        return pl.BlockSpec((tm, w), lambda j, i: (i, 0))

    def gate_cols(b):
        return pl.BlockSpec((R, tn), lambda j, i: (0, j + b * nj))

    def cols(k):
        return pl.BlockSpec((k, tn), lambda j, i: (0, j))

    return pl.pallas_call(
        _merge_kernel,
        grid=(nj, M // tm),
        in_specs=[rows(R), rows(y_att.shape[1]), rows(y_ssd.shape[1]), rows(y_hgrn.shape[1]),
                  gate_cols(0), gate_cols(1), gate_cols(2),
                  cols(w_att.shape[0]), cols(w_ssd.shape[0]), cols(w_hgrn.shape[0])],
        out_specs=pl.BlockSpec((tm, tn), lambda j, i: (i, j)),
        out_shape=jax.ShapeDtypeStruct((M, D), bf16),
        compiler_params=_cparams("parallel", "parallel"),
        name="merge_branches",
    )(gc, y_att, y_ssd, y_hgrn, w_gate, w_gate, w_gate, w_att, w_ssd, w_hgrn)


def _softmax_parts(parts, sink):
    m = sink
    for s in parts:
        m = jnp.maximum(m, jnp.max(s, axis=-1, keepdims=True))
    ps = [jnp.exp(s - m) for s in parts]
    den = jnp.exp(sink - m)
    for p in ps:
        den = den + jnp.sum(p, axis=-1, keepdims=True)
    return ps, 1.0 / den


def _attn_ctx_kernel(sink_ref, q_ref, k_ref, v_ref, o_ref):
    g = N_HEADS // N_KV
    scale = HEAD_DIM ** -0.5
    for kv in range(N_KV):
        sl = slice(kv * HEAD_DIM, (kv + 1) * HEAD_DIM)
        k = k_ref[:, sl].astype(bf16)
        v = v_ref[:, sl].astype(bf16)
        for gi in range(g):
            h = kv * g + gi
            hs = slice(h * HEAD_DIM, (h + 1) * HEAD_DIM)
            q = (q_ref[:, hs] * scale).astype(bf16)
            s = lax.dot_general(q, k, (((1,), (1,)), ((), ())), preferred_element_type=f32)
            (p,), inv = _softmax_parts([s], sink_ref[h])
            o = jnp.dot(p.astype(bf16), v, preferred_element_type=f32) * inv
            o_ref[:, hs] = o.astype(o_ref.dtype)


def attention_context(q, kv, sink):
    att_w, kv_w = q.shape[1], kv.shape[1] // 2
    return pl.pallas_call(
        _attn_ctx_kernel,
        grid=(BATCH,),
        in_specs=[pl.BlockSpec(memory_space=pltpu.SMEM),
                  pl.BlockSpec((SEQ, att_w), lambda b: (b, 0)),
                  pl.BlockSpec((SEQ, kv_w), lambda b: (b, 0)),
                  pl.BlockSpec((SEQ, kv_w), lambda b: (b, 1))],
        out_specs=pl.BlockSpec((SEQ, att_w), lambda b: (b, 0)),
        out_shape=jax.ShapeDtypeStruct((BATCH * SEQ, att_w), bf16),
        compiler_params=_cparams("parallel"),
        name="attention_context",
    )(sink, q, kv, kv)


def _rope_kernel(n_q, n_k, q_ref, kv_ref, cos_ref, sin_ref, qo_ref, kvo_ref):
    cos = cos_ref[...]
    sin = sin_ref[...]
    quarter = HEAD_DIM // 4
    lane = lax.broadcasted_iota(jnp.int32, cos.shape, 1)
    first = (lane % (2 * quarter)) < quarter

    def rot(x):
        swapped = jnp.where(first, pltpu.roll(x, LANE - quarter, axis=1), pltpu.roll(x, quarter, axis=1))
        return x * cos + swapped * sin

    scale = HEAD_DIM ** -0.5
    for j in range(n_q):
        sl = slice(j * LANE, (j + 1) * LANE)
        qo_ref[:, sl] = (rot(q_ref[:, sl]) * scale).astype(qo_ref.dtype)
    for j in range(n_k):
        sl = slice(j * LANE, (j + 1) * LANE)
        kvo_ref[:, sl] = rot(kv_ref[:, sl]).astype(kvo_ref.dtype)
    for j in range(n_k, 2 * n_k):
        sl = slice(j * LANE, (j + 1) * LANE)
        kvo_ref[:, sl] = kv_ref[:, sl].astype(kvo_ref.dtype)


def rope_latent(q, kv, cos, sin, row0):
    att_w, kv2 = q.shape[1], kv.shape[1]
    nl = DEC_BATCH * DEC_SEQ
    tm = _pick(DEC_SEQ, (512, 256, 128))
    off = row0 // tm
    per_seq = DEC_SEQ // tm
    tab = pl.BlockSpec((tm, LANE), lambda i: (i % per_seq, 0))
    return pl.pallas_call(
        functools.partial(_rope_kernel, att_w // LANE, kv2 // 2 // LANE),
        grid=(nl // tm,),
        in_specs=[pl.BlockSpec((tm, att_w), lambda i: (i + off, 0)),
                  pl.BlockSpec((tm, kv2), lambda i: (i + off, 0)), tab, tab],
        out_specs=[pl.BlockSpec((tm, att_w), lambda i: (i, 0)), pl.BlockSpec((tm, kv2), lambda i: (i, 0))],
        out_shape=[jax.ShapeDtypeStruct((nl, att_w), bf16), jax.ShapeDtypeStruct((nl, kv2), bf16)],
        compiler_params=_cparams("parallel"),
        name="rope_latent",
    )(q, kv, cos, sin)


def _attn_lat_kernel(nb, sink_ref, q_ref, kvp_ref, kvc_ref, kvn_ref, kc_ref, vc_ref, o_ref):
    n = pl.program_id(1)
    g = N_HEADS // N_KV
    kv_w = N_KV * HEAD_DIM
    qi = lax.broadcasted_iota(jnp.int32, (BLOCK, 3 * BLOCK), 0)
    kj = lax.broadcasted_iota(jnp.int32, (BLOCK, 3 * BLOCK), 1)
    rel = kj - BLOCK - qi
    valid = (jnp.abs(rel) <= WINDOW)
    valid = valid & ((kj >= BLOCK) | (n > 0)) & ((kj < 2 * BLOCK) | (n < nb - 1))
    kvw = jnp.concatenate([kvp_ref[...], kvc_ref[...], kvn_ref[...]], axis=0)
    for kv in range(N_KV):
        sl = slice(kv * HEAD_DIM, (kv + 1) * HEAD_DIM)
        k_loc = kvw[:, sl]
        v_loc = kvw[:, kv_w + kv * HEAD_DIM: kv_w + (kv + 1) * HEAD_DIM]
        k_ctx = kc_ref[:, sl]
        v_ctx = vc_ref[:, sl]
        for gi in range(g):
            h = kv * g + gi
            hs = slice(h * HEAD_DIM, (h + 1) * HEAD_DIM)
            q = q_ref[:, hs]
            s_ctx = lax.dot_general(q, k_ctx, (((1,), (1,)), ((), ())), preferred_element_type=f32)
            s_loc = lax.dot_general(q, k_loc, (((1,), (1,)), ((), ())), preferred_element_type=f32)
            s_loc = jnp.where(valid, s_loc, -1e30)
            (p_ctx, p_loc), inv = _softmax_parts([s_ctx, s_loc], sink_ref[h])
            o = (jnp.dot(p_ctx.astype(bf16), v_ctx, preferred_element_type=f32)
                 + jnp.dot(p_loc.astype(bf16), v_loc, preferred_element_type=f32)) * inv
            o_ref[:, hs] = o.astype(o_ref.dtype)


def attention_latent(q_rot, kv_rot, k_ctx, v_ctx, sink):
    att_w, kv2 = q_rot.shape[1], kv_rot.shape[1]
    nb = DEC_SEQ // BLOCK
    past = k_ctx.shape[1]
    kvspec = lambda f: pl.BlockSpec((BLOCK, kv2), lambda b, n: (b * nb + f(n), 0))
    cspec = pl.BlockSpec((None, past, kv2 // 2), lambda b, n: (b, 0, 0))
    return pl.pallas_call(
        functools.partial(_attn_lat_kernel, nb),
        grid=(DEC_BATCH, nb),
        in_specs=[pl.BlockSpec(memory_space=pltpu.SMEM),
                  pl.BlockSpec((BLOCK, att_w), lambda b, n: (b * nb + n, 0)),
                  kvspec(lambda n: jnp.maximum(n - 1, 0)), kvspec(lambda n: n),
                  kvspec(lambda n: jnp.minimum(n + 1, nb - 1)), cspec, cspec],
        out_specs=pl.BlockSpec((BLOCK, att_w), lambda b, n: (b * nb + n, 0)),
        out_shape=jax.ShapeDtypeStruct((DEC_BATCH * DEC_SEQ, att_w), bf16),
        compiler_params=_cparams("parallel", "parallel"),
        name="attention_latent",
    )(sink, q_rot, kv_rot, kv_rot, kv_rot, k_ctx, v_ctx)


def _ssd_kernel(reverse, final, nc, *refs):
    (xp_ref, xc_ref, xn_ref, dt_ref, cw_ref, cb_ref, dtb_ref, a_ref, h0_ref) = refs[:9]
    if final:
        dskip_ref, z_ref, yprev_ref, norm_ref, y_ref, hfin_ref, st_ref, ybuf_ref = refs[9:]
    else:
        y_ref, hfin_ref, st_ref = refs[9:]
        ybuf_ref = y_ref
    L = SSD_CHUNK
    heads = SSD_W // SSD_HEAD_DIM
    hg = heads // SSD_GROUPS
    P = SSD_HEAD_DIM
    halo = SUBLANE
    c = pl.program_id(1)
    cc = (nc - 1 - c) if reverse else c

    @pl.when(c == 0)
    def _():
        st_ref[...] = h0_ref[...]

    xp = jnp.where(cc > 0, xp_ref[...], 0.0)
    xn = jnp.where(cc < nc - 1, xn_ref[...], 0.0)
    ext = jnp.concatenate([xp, xc_ref[...], xn], axis=0)
    pad = SSD_CONV // 2
    acc = cb_ref[...] + ext[halo - pad: halo - pad + L] * cw_ref[0:1, :]
    for k in range(1, SSD_CONV):
        acc = acc + ext[halo - pad + k: halo - pad + k + L] * cw_ref[k:k + 1, :]
    xbc = _silu(acc)

    raw = dt_ref[...] + dtb_ref[...]
    dt = jnp.maximum(raw, 0.0) + jnp.log(1.0 + jnp.exp(-jnp.abs(raw)))
    la = dt * a_ref[...]
    ri = lax.broadcasted_iota(jnp.int32, (L, L), 0)
    ci = lax.broadcasted_iota(jnp.int32, (L, L), 1)
    causal = (ci >= ri) if reverse else (ci <= ri)
    tri = jnp.where(causal, 1.0, 0.0).astype(f32)
    acum = jnp.dot(tri, la, preferred_element_type=f32, precision=HIGHEST)
    acum_t = acum.T
    last = 0 if reverse else L - 1
    lane0 = heads if reverse else 0

    for g in range(SSD_GROUPS):
        bm = xbc[:, SSD_W + g * SSD_STATE: SSD_W + (g + 1) * SSD_STATE].astype(bf16)
        cm_off = SSD_W + SSD_GROUPS * SSD_STATE
        cm = xbc[:, cm_off + g * SSD_STATE: cm_off + (g + 1) * SSD_STATE].astype(bf16)
        cb = lax.dot_general(cm, bm, (((1,), (1,)), ((), ())), preferred_element_type=f32)
        for hh in range(hg):
            h = g * hg + hh
            ln = lane0 + h
            col = acum[:, ln:ln + 1]
            row = acum_t[ln:ln + 1, :]
            tot = acum[last:last + 1, ln:ln + 1]
            decay = jnp.exp(jnp.where(causal, col - row, -1e30))
            xdt = xbc[:, h * P:(h + 1) * P] * dt[:, ln:ln + 1]
            y = jnp.dot((cb * decay).astype(bf16), xdt.astype(bf16), preferred_element_type=f32)
            h_in = st_ref[h]
            y = y + lax.dot_general(cm, h_in.astype(bf16), (((1,), (1,)), ((), ())),
                                    preferred_element_type=f32) * jnp.exp(col)
            st = lax.dot_general((xdt * jnp.exp(tot - col)).astype(bf16), bm, (((0,), (0,)), ((), ())),
                                 preferred_element_type=f32)
            st_ref[h] = jnp.exp(tot) * h_in + st
            ybuf_ref[:, h * P:(h + 1) * P] = y

    @pl.when(c == nc - 1)
    def _():
        hfin_ref[...] = st_ref[...]

    if final:
        yt = (ybuf_ref[...] + yprev_ref[...] + dskip_ref[...] * xbc[:, :SSD_W]) * _silu(z_ref[...])
        yn = yt * lax.rsqrt(jnp.mean(yt * yt, axis=-1, keepdims=True) + EPS) * norm_ref[...]
        y_ref[...] = yn.astype(y_ref.dtype)


def ssd_pass(xbc, z, dtr, conv_w, conv_b, dt_bias, a_neg, h0, n_seq, T, row0, reverse,
             dskip=None, y_prev=None, norm_g=None):
    final = y_prev is not None
    L = SSD_CHUNK
    nc = T // L
    cc_w = xbc.shape[1]
    heads = SSD_W // SSD_HEAD_DIM
    hb = L // SUBLANE
    c0 = row0 // L

    def chunk(c):
        return (nc - 1 - c) if reverse else c

    def cur(w):
        return pl.BlockSpec((L, w), lambda s, c: (c0 + s * nc + chunk(c), 0))

    def out_rows(w):
        return pl.BlockSpec((L, w), lambda s, c: (s * nc + chunk(c), 0))

    prev = pl.BlockSpec((SUBLANE, cc_w), lambda s, c: (jnp.maximum((c0 + s * nc + chunk(c)) * hb - 1, 0), 0))
    last_blk = (xbc.shape[0] // SUBLANE) - 1
    nxt = pl.BlockSpec((SUBLANE, cc_w), lambda s, c: (jnp.minimum((c0 + s * nc + chunk(c) + 1) * hb, last_blk), 0))
    full = lambda a: pl.BlockSpec(a.shape, lambda s, c: (0,) * a.ndim)
    st_spec = pl.BlockSpec((None, heads, SSD_HEAD_DIM, SSD_STATE), lambda s, c: (s, 0, 0, 0))
    in_specs = [prev, cur(cc_w), nxt, cur(LANE), full(conv_w), full(conv_b), full(dt_bias), full(a_neg), st_spec]
    args = [xbc, xbc, xbc, dtr, conv_w, conv_b, dt_bias, a_neg, h0]
    scratch = [pltpu.VMEM((heads, SSD_HEAD_DIM, SSD_STATE), f32)]
    if final:
        in_specs += [full(dskip), cur(SSD_W), out_rows(SSD_W), full(norm_g)]
        args += [dskip, z, y_prev, norm_g]
        scratch.append(pltpu.VMEM((L, SSD_W), f32))
    return pl.pallas_call(
        functools.partial(_ssd_kernel, reverse, final, nc),
        grid=(n_seq, nc),
        in_specs=in_specs,
        out_specs=[out_rows(SSD_W), st_spec],
        out_shape=[jax.ShapeDtypeStruct((n_seq * T, SSD_W), bf16 if final else f32),
                   jax.ShapeDtypeStruct((n_seq, heads, SSD_HEAD_DIM, SSD_STATE), f32)],
        scratch_shapes=scratch,
        compiler_params=_cparams("parallel", "arbitrary"),
        name="ssd_final" if final else "ssd_first",
    )(*args)


def _hgrn_kernel(reverse, final, nblk, *refs):
    hq_ref, hf_ref, hi_ref, lb_ref, s0_ref = refs[:5]
    if final:
        hg_ref, oprev_ref, norm_ref, y_ref, sfin_ref, st_ref, q_s, k_s, bc_s, o_s = refs[5:]
    else:
        y_ref, sfin_ref, st_ref, q_s, k_s, bc_s = refs[5:]
        o_s = y_ref
    R = HGRN_ROWS
    C = HGRN_CHUNK
    nsub = R // C
    heads = HGRN_W // HGRN_DK
    DK = HGRN_DK
    b = pl.program_id(1)

    @pl.when(b == 0)
    def _():
        for h in range(heads):
            st_ref[h] = s0_ref[h].T

    r = hf_ref[...]
    lb = lb_ref[...]
    e = jnp.exp(-jnp.abs(r))
    inv = 1.0 / (1.0 + e)
    sig_pos = jnp.where(r >= 0, inv, e * inv)
    sig_neg = jnp.where(r >= 0, e * inv, inv)
    logf = jnp.log(lb + (1.0 - lb) * sig_pos)
    k_s[...] = (1.0 - lb) * sig_neg
    q_s[...] = _silu(hq_ref[...])
    ri = lax.broadcasted_iota(jnp.int32, (R, R), 0)
    ci = lax.broadcasted_iota(jnp.int32, (R, R), 1)
    same = (ri // C) == (ci // C)
    tri = jnp.where(same & ((ci >= ri) if reverse else (ci <= ri)), 1.0, 0.0).astype(f32)
    bc_s[...] = jnp.dot(tri, logf, preferred_element_type=f32, precision=HIGHEST)

    ii = lax.broadcasted_iota(jnp.int32, (C, 1), 0)
    last = 0 if reverse else C - 1

    def sub_chunk(t, carry):
        sc = (nsub - 1 - t) if reverse else t
        r0 = pl.multiple_of(sc * C, C)
        rows = pl.ds(r0, C)
        for h in range(heads):
            hs = slice(h * DK, (h + 1) * DK)
            q = q_s[rows, hs]
            k = k_s[rows, hs]
            bc = bc_s[rows, hs]
            v = hi_ref[rows, hs]
            st = st_ref[h]
            o = lax.dot_general((q * jnp.exp(bc)).astype(bf16), st.astype(bf16), (((1,), (1,)), ((), ())),
                                preferred_element_type=f32)
            for j in range(C):
                ej = q * jnp.exp(jnp.minimum(bc - bc[j:j + 1, :], 0.0)) * k[j:j + 1, :]
                a = jnp.sum(ej, axis=-1, keepdims=True)
                keep = (ii <= j) if reverse else (ii >= j)
                o = o + jnp.where(keep, a, 0.0) * v[j:j + 1, :]
            o_s[rows, hs] = o
            bl = bc[last:last + 1, :]
            kd = (k * jnp.exp(bl - bc)).astype(bf16)
            upd = lax.dot_general(v.astype(bf16), kd, (((0,), (0,)), ((), ())), preferred_element_type=f32)
            st_ref[h] = st * jnp.exp(bl) + upd
        return carry

    lax.fori_loop(0, nsub, sub_chunk, 0)

    @pl.when(b == nblk - 1)
    def _():
        for h in range(heads):
            sfin_ref[h] = st_ref[h].T

    if final:
        ot = o_s[...] + oprev_ref[...]
        on = ot * lax.rsqrt(jnp.mean(ot * ot, axis=-1, keepdims=True) + EPS) * norm_ref[...]
        y_ref[...] = (on * _silu(hg_ref[...])).astype(y_ref.dtype)


def hgrn_pass(ph, f_col, lb, s0, n_seq, T, row0, reverse, o_prev=None, norm_g=None):
    final = o_prev is not None
    R = HGRN_ROWS
    nblk = T // R
    heads = HGRN_W // HGRN_DK
    b0 = row0 // R

    def blk(b):
        return (nblk - 1 - b) if reverse else b

    def col(j):
        return pl.BlockSpec((R, HGRN_W), lambda s, b: (b0 + s * nblk + blk(b), j))

    out_rows = pl.BlockSpec((R, HGRN_W), lambda s, b: (s * nblk + blk(b), 0))
    full = lambda a: pl.BlockSpec(a.shape, lambda s, b: (0,) * a.ndim)
    st_spec = pl.BlockSpec((None, heads, HGRN_DK, HGRN_DV), lambda s, b: (s, 0, 0, 0))
    in_specs = [col(0), col(f_col), col(3), full(lb), st_spec]
    args = [ph, ph, ph, lb, s0]
    scratch = [pltpu.VMEM((heads, HGRN_DV, HGRN_DK), f32)] + [pltpu.VMEM((R, HGRN_W), f32)] * 3
    if final:
        in_specs += [col(4), out_rows, full(norm_g)]
        args += [ph, o_prev, norm_g]
        scratch.append(pltpu.VMEM((R, HGRN_W), f32))
    return pl.pallas_call(
        functools.partial(_hgrn_kernel, reverse, final, nblk),
        grid=(n_seq, nblk),
        in_specs=in_specs,
        out_specs=[out_rows, st_spec],
        out_shape=[jax.ShapeDtypeStruct((n_seq * T, HGRN_W), bf16 if final else f32),
                   jax.ShapeDtypeStruct((n_seq, heads, HGRN_DK, HGRN_DV), f32)],
        scratch_shapes=scratch,
        compiler_params=_cparams("parallel", "arbitrary"),
        name="hgrn_final" if final else "hgrn_first",
    )(*args)


def _row_copy(src_hbm, dst, sem, src_row, dst_row):
    return pltpu.make_async_copy(src_hbm.at[pl.ds(src_row, 1), :], dst.at[pl.ds(dst_row, 1), :], sem)


def _moe_gather_kernel(idx_ref, h_hbm, o_ref, buf, sem):
    n = buf.shape[0]

    def start(r, c):
        _row_copy(h_hbm, buf, sem, idx_ref[0, 0, r], r).start()
        return c

    def wait(r, c):
        _row_copy(h_hbm, buf, sem, 0, r).wait()
        return c

    lax.fori_loop(0, n, start, 0)
    lax.fori_loop(0, n, wait, 0)
    o_ref[...] = buf[...].astype(o_ref.dtype)


def moe_gather(h2, slot_tok):
    D = h2.shape[1]
    n_tiles, _, tm = slot_tok.shape
    return pl.pallas_call(
        _moe_gather_kernel,
        grid=(n_tiles,),
        in_specs=[pl.BlockSpec((1, 1, tm), lambda t: (t, 0, 0), memory_space=pltpu.SMEM),
                  pl.BlockSpec(memory_space=pl.ANY)],
        out_specs=pl.BlockSpec((tm, D), lambda t: (t, 0)),
        out_shape=jax.ShapeDtypeStruct((n_tiles * tm, D), bf16),
        scratch_shapes=[pltpu.VMEM((tm, D), h2.dtype), pltpu.SemaphoreType.DMA(())],
        compiler_params=_cparams("arbitrary"),
        name="moe_gather",
    )(slot_tok, h2)


def _moe_up_kernel(te_ref, x_ref, wg_ref, wu_ref, o_ref):
    x = x_ref[...]
    a = jnp.dot(x, wg_ref[...], preferred_element_type=f32)
    u = jnp.dot(x, wu_ref[...], preferred_element_type=f32)
    o_ref[...] = (_silu(a) * u).astype(o_ref.dtype)


def moe_up(tile_e, xs, wg, wu):
    P, D = xs.shape
    E, _, F = wg.shape
    tm = MOE_TILE
    wspec = pl.BlockSpec((None, D, F), lambda t, te: (te[t], 0, 0), pipeline_mode=pl.Buffered(1))
    return pl.pallas_call(
        _moe_up_kernel,
        grid_spec=pltpu.PrefetchScalarGridSpec(
            num_scalar_prefetch=1, grid=(P // tm,),
            in_specs=[pl.BlockSpec((tm, D), lambda t, te: (t, 0)), wspec, wspec],
            out_specs=pl.BlockSpec((tm, F), lambda t, te: (t, 0))),
        out_shape=jax.ShapeDtypeStruct((P, F), bf16),
        compiler_params=_cparams("arbitrary"),
        name="moe_up",
    )(tile_e, xs, wg, wu)


def _moe_down_kernel(te_ref, a_ref, w_ref, o_ref):
    o_ref[...] = jnp.dot(a_ref[...], w_ref[...], preferred_element_type=f32)


def moe_down(tile_e, act, wd):
    P, F = act.shape
    E, _, D = wd.shape
    tm = MOE_TILE
    return pl.pallas_call(
        _moe_down_kernel,
        grid_spec=pltpu.PrefetchScalarGridSpec(
            num_scalar_prefetch=1, grid=(P // tm,),
            in_specs=[pl.BlockSpec((tm, F), lambda t, te: (t, 0)),
                      pl.BlockSpec((None, F, D), lambda t, te: (te[t], 0, 0), pipeline_mode=pl.Buffered(1))],
            out_specs=pl.BlockSpec((tm, D), lambda t, te: (t, 0))),
        out_shape=jax.ShapeDtypeStruct((P, D), f32),
        compiler_params=_cparams("arbitrary"),
        name="moe_down",
    )(tile_e, act, wd)


def _moe_combine_kernel(pos_ref, w_ref, x_ref, g_ref, ys_hbm, o_ref, buf, sem):
    n = x_ref.shape[0]

    def start(r, c):
        for kk in range(TOP_K):
            _row_copy(ys_hbm, buf.at[kk], sem, pos_ref[0, kk, r], r).start()
        return c

    def wait(r, c):
        for kk in range(TOP_K):
            _row_copy(ys_hbm, buf.at[kk], sem, 0, r).wait()
        return c

    lax.fori_loop(0, n, start, 0)
    lax.fori_loop(0, n, wait, 0)
    w = w_ref[...]
    y = w[:, 0:1] * buf[0]
    for kk in range(1, TOP_K):
        y = y + w[:, kk:kk + 1] * buf[kk]
    o_ref[...] = x_ref[...] + g_ref[...] * y


def moe_combine(x, gate, ys, pos, top_w):
    M, D = x.shape
    tm = pos.shape[2]
    return pl.pallas_call(
        _moe_combine_kernel,
        grid=(M // tm,),
        in_specs=[pl.BlockSpec((1, TOP_K, tm), lambda i: (i, 0, 0), memory_space=pltpu.SMEM),
                  pl.BlockSpec((tm, TOP_K), lambda i: (i, 0)),
                  pl.BlockSpec((tm, D), lambda i: (i, 0)),
                  pl.BlockSpec((None, 1, D), lambda i: (_group_of_tile(i, tm), 0, 0)),
                  pl.BlockSpec(memory_space=pl.ANY)],
        out_specs=pl.BlockSpec((tm, D), lambda i: (i, 0)),
        out_shape=jax.ShapeDtypeStruct((M, D), f32),
        scratch_shapes=[pltpu.VMEM((TOP_K, tm, D), f32), pltpu.SemaphoreType.DMA(())],
        compiler_params=_cparams("arbitrary"),
        name="moe_combine",
    )(pos, top_w, x, gate, ys)


def moe_ffn(x, h2, gate, w_router, wg, wu, wd):
    M, D = h2.shape
    E = w_router.shape[1]
    wr = jnp.zeros((D, LANE), f32).at[:, :E].set(w_router)
    logits = router_logits(h2, wr)[:, :E]
    top_v, top_i = lax.top_k(logits, TOP_K)
    top_w = jax.nn.softmax(top_v, axis=-1)
    nk = M * TOP_K
    e_flat = top_i.reshape(-1).astype(jnp.int32)
    order = jnp.argsort(e_flat).astype(jnp.int32)
    e_sorted = e_flat[order]
    counts = jnp.sum((e_flat[:, None] == jnp.arange(E, dtype=jnp.int32)[None, :]).astype(jnp.int32), axis=0)
    padded = (counts + MOE_TILE - 1) // MOE_TILE * MOE_TILE
    start = jnp.cumsum(counts) - counts
    pend = jnp.cumsum(padded)
    pstart = pend - padded
    pos_sorted = pstart[e_sorted] + jnp.arange(nk, dtype=jnp.int32) - start[e_sorted]
    n_tiles = -(-nk // MOE_TILE) + E
    slot_tok = jnp.zeros((n_tiles * MOE_TILE,), jnp.int32).at[pos_sorted].set(order // TOP_K)
    pos = jnp.zeros((nk,), jnp.int32).at[order].set(pos_sorted)
    tile_e = jnp.minimum(jnp.searchsorted(pend, jnp.arange(n_tiles, dtype=jnp.int32) * MOE_TILE, side='right'),
                         E - 1).astype(jnp.int32)
    xs = moe_gather(h2, slot_tok.reshape(n_tiles, 1, MOE_TILE))
    act = moe_up(tile_e, xs, wg, wu)
    ys = moe_down(tile_e, act, wd)
    tc = _row_tile((256, 128))
    pos_t = pos.reshape(M // tc, tc, TOP_K).transpose(0, 2, 1)
    return moe_combine(x, gate, ys, pos_t, top_w)


def _rope_tables():
    rows = DEC_SEQ // GRID_W
    row = jnp.repeat(jnp.arange(rows), GRID_W)
    col = jnp.tile(jnp.arange(GRID_W), rows)
    quarter = HEAD_DIM // 4
    inv = ROPE_THETA ** (-jnp.arange(quarter, dtype=f32) / quarter)
    ar = row.astype(f32)[:, None] * inv[None, :]
    ac = col.astype(f32)[:, None] * inv[None, :]
    cos = jnp.concatenate([jnp.cos(ar), jnp.cos(ar), jnp.cos(ac), jnp.cos(ac)], axis=-1)
    sin = jnp.concatenate([-jnp.sin(ar), jnp.sin(ar), -jnp.sin(ac), jnp.sin(ac)], axis=-1)
    rep = LANE // HEAD_DIM
    return jnp.tile(cos, (1, rep)), jnp.tile(sin, (1, rep))


def kernel(x_prompt, x_sample, cache_k, cache_v, state_ssd, state_hgrn, c, c_ctx, w_ada, b_ada, norm1_g, norm2_g, w_in, attn_sink, ssd_conv_w, ssd_conv_b, ssd_dt_bias, ssd_a_log, ssd_d, ssd_norm_g, hgrn_lb, hgrn_norm_g, w_gate, w_br_att, w_br_ssd, w_br_hgrn, w_out, ffn_wg, ffn_wu, ffn_wd, router_w, moe_wg, moe_wu, moe_wd, final_g):
    att_w, kv_w, ssd_heads, conv_ch, hgrn_heads, nc, nl = _dims()
    D = D_MODEL
    G = 1 + DEC_BATCH
    g_pad = -(-G // SUBLANE) * SUBLANE
    x = jnp.concatenate([x_prompt.reshape(nc, D), x_sample.reshape(nl, D)], axis=0)
    m = jnp.zeros((g_pad, D), f32).at[0].set(c_ctx).at[1:G].set(c)
    mods = ada_mod(m, w_ada, b_ada)
    sm = jax.nn.softmax(hgrn_lb.astype(f32), axis=1)
    lb_all = jnp.cumsum(sm, axis=1) - sm[:, :1]
    cos, sin = _rope_tables()

    cuts = np.cumsum([0, att_w, kv_w, kv_w, SSD_W, conv_ch, 2 * ssd_heads, 5 * HGRN_W, GATE_RANK])
    ks, vs, hs, ss = [], [], [], []
    for l in range(DEPTH):
        mod = mods[l].reshape(g_pad, 6, 1, D)
        sh1, sc1, g1, sh2, sc2, g2 = [mod[:, j] for j in range(6)]
        wl = w_in[l]
        seg = lambda j: wl[:, cuts[j]:cuts[j + 1]].astype(bf16)
        w_dt = jnp.zeros((D, LANE), bf16).at[:, :2 * ssd_heads].set(seg(5))

        h = norm_mod(x, norm1_g[l], sc1, sh1)
        q = matmul(h, seg(0), f32, "proj_q")
        kv = matmul(h, wl[:, cuts[1]:cuts[3]].astype(bf16), f32, "proj_kv")
        z = matmul(h, seg(3), f32, "proj_z")
        xbc = matmul(h, seg(4), f32, "proj_xbc")
        dtr = matmul(h, w_dt, f32, "proj_dt")
        ph = matmul(h, seg(6), f32, "proj_hgrn")
        gc = matmul(h, seg(7), bf16, "proj_gate")

        y_att_c = attention_context(q, kv, attn_sink[l])
        q_rot, kv_rot = rope_latent(q, kv, cos, sin, nc)
        y_att_l = attention_latent(q_rot, kv_rot,
                                   cache_k[:, l].reshape(DEC_BATCH, PAST_LEN, kv_w).astype(bf16),
                                   cache_v[:, l].reshape(DEC_BATCH, PAST_LEN, kv_w).astype(bf16), attn_sink[l])
        y_att = jnp.concatenate([y_att_c, y_att_l], axis=0)
        ks.append(kv[:nc, :kv_w].reshape(BATCH, SEQ, N_KV, HEAD_DIM))
        vs.append(kv[:nc, kv_w:].reshape(BATCH, SEQ, N_KV, HEAD_DIM))

        lanes = jnp.zeros((1, LANE), f32)
        dt_bias = lanes.at[0, :2 * ssd_heads].set(ssd_dt_bias[l].reshape(-1))
        a_neg = lanes.at[0, :2 * ssd_heads].set(-jnp.exp(ssd_a_log[l].astype(f32)).reshape(-1))
        dskip = jnp.repeat(ssd_d[l].astype(f32), SSD_HEAD_DIM).reshape(1, SSD_W)
        cw, cb = ssd_conv_w[l], ssd_conv_b[l].reshape(1, conv_ch)
        nrm = ssd_norm_g[l].reshape(1, SSD_W)
        zero_ssd = jnp.zeros((BATCH, ssd_heads, SSD_HEAD_DIM, SSD_STATE), f32)
        y_ssd, h_ssd = [], []
        for (n_seq, T, row0, h0f, h0b) in ((BATCH, SEQ, 0, zero_ssd, zero_ssd),
                                           (DEC_BATCH, DEC_SEQ, nc, state_ssd[:, l, 0], state_ssd[:, l, 1])):
            yf, hf = ssd_pass(xbc, z, dtr, cw, cb, dt_bias, a_neg, h0f, n_seq, T, row0, False)
            yb, hb = ssd_pass(xbc, z, dtr, cw, cb, dt_bias, a_neg, h0b, n_seq, T, row0, True,
                              dskip=dskip, y_prev=yf, norm_g=nrm)
            y_ssd.append(yb)
            h_ssd.append((hf, hb))
        hs.append(jnp.stack(h_ssd[0], axis=1))
        y_ssd = jnp.concatenate(y_ssd, axis=0)

        lb = lb_all[:, l]
        hn = hgrn_norm_g[l].reshape(1, HGRN_W)
        zero_h = jnp.zeros((BATCH, hgrn_heads, HGRN_DK, HGRN_DV), f32)
        y_hgrn, s_hgrn = [], []
        for (n_seq, T, row0, s0f, s0b) in ((BATCH, SEQ, 0, zero_h, zero_h),
                                           (DEC_BATCH, DEC_SEQ, nc, state_hgrn[:, l, 0], state_hgrn[:, l, 1])):
            of, sf = hgrn_pass(ph, 1, lb[0:1], s0f, n_seq, T, row0, False)
            ob, sb = hgrn_pass(ph, 2, lb[1:2], s0b, n_seq, T, row0, True, o_prev=of, norm_g=hn)
            y_hgrn.append(ob)
            s_hgrn.append((sf, sb))
        ss.append(jnp.stack(s_hgrn[0], axis=1))
        y_hgrn = jnp.concatenate(y_hgrn, axis=0)

        merged = merge_branches(gc, y_att, y_ssd, y_hgrn, w_gate[l].astype(bf16), w_br_att[l].astype(bf16),
                                w_br_ssd[l].astype(bf16), w_br_hgrn[l].astype(bf16))
        x = matmul_residual(merged, w_out[l].astype(bf16), x, g1, "out_proj")

        i = l // 2
        if l % 2 == 0:
            h2 = norm_mod(x, norm2_g[l], sc2, sh2)
            act = glu_up(h2, ffn_wg[i].astype(bf16), ffn_wu[i].astype(bf16))
            x = matmul_residual(act, ffn_wd[i].astype(bf16), x, g2, "ffn_down")
        else:
            h2 = norm_mod(x, norm2_g[l], sc2, sh2, out_dtype=f32)
            x = moe_ffn(x, h2, g2, router_w[i], moe_wg[i].astype(bf16), moe_wu[i].astype(bf16),
                        moe_wd[i].astype(bf16))

    y = norm_mod(x, final_g, out_dtype=f32)
    y_prompt = y[:nc].reshape(BATCH, SEQ, D)
    y_sample = y[nc:].reshape(DEC_BATCH, DEC_SEQ, D)
    return (y_prompt, y_sample, jnp.stack(ks, axis=1), jnp.stack(vs, axis=1),
            jnp.stack(hs, axis=1), jnp.stack(ss, axis=1))
```

```python
import functools

import numpy as np
import jax
import jax.numpy as jnp
from jax import lax
from jax.experimental import pallas as pl
from jax.experimental.pallas import tpu as pltpu

f32 = jnp.float32
bf16 = jnp.bfloat16

D_MODEL = 4096
BATCH = 16
SEQ = 256
DEPTH = 4
DEC_BATCH = 4
DEC_SEQ = 4096
PAST_LEN = 256
GRID_W = 64
N_HEADS = 16
N_KV = 4
HEAD_DIM = 64
WINDOW = 128
BLOCK = 128
ROPE_THETA = 10000.0
SSD_W = 1024
SSD_HEAD_DIM = 64
SSD_GROUPS = 2
SSD_STATE = 128
SSD_CONV = 5
SSD_CHUNK = 128
HGRN_W = 1024
HGRN_DK = 128
HGRN_DV = 128
HGRN_CHUNK = 16
GATE_RANK = 512
D_FF = 5632
N_EXPERTS = 8
TOP_K = 2
EXPERT_FF = 1408
EPS = 1e-6

LANE = 128
SUBLANE = 8
VMEM_LIMIT = 56 * 1024 * 1024
MOE_TILE = 512
HGRN_ROWS = 128
DT_PAD = 2 * LANE
HIGHEST = lax.Precision.HIGHEST


def _cparams(*sem):
    return pltpu.CompilerParams(dimension_semantics=sem, vmem_limit_bytes=VMEM_LIMIT)


def _pick(n, prefs):
    for p in prefs:
        if n % p == 0:
            return p
    return n


def _silu(x):
    return x * (1.0 / (1.0 + jnp.exp(-x)))


def _sigmoid(x):
    return 1.0 / (1.0 + jnp.exp(-x))


def _dims():
    att_w = N_HEADS * HEAD_DIM
    kv_w = N_KV * HEAD_DIM
    ssd_heads = SSD_W // SSD_HEAD_DIM
    conv_ch = SSD_W + 2 * SSD_GROUPS * SSD_STATE
    hgrn_heads = HGRN_W // HGRN_DK
    nc = BATCH * SEQ
    nl = DEC_BATCH * DEC_SEQ
    return att_w, kv_w, ssd_heads, conv_ch, hgrn_heads, nc, nl


def _group_of_tile(i, tm):
    nc = BATCH * SEQ
    r = i * tm
    return jnp.where(r < nc, 0, 1 + (r - nc) // DEC_SEQ)


def _row_tile(prefs=(512, 256, 128)):
    nc = BATCH * SEQ
    return _pick(int(np.gcd(nc, DEC_SEQ)), prefs)


def _ada_kernel(m_ref, w_ref, b_ref, o_ref):
    a = _silu(m_ref[...]).astype(bf16)
    o_ref[...] = jnp.dot(a, w_ref[...].astype(bf16), preferred_element_type=f32) + b_ref[...]


def ada_mod(m_pad, w_ada, b_ada):
    L, D, N = w_ada.shape
    G = m_pad.shape[0]
    tn = _pick(N, (512, 256, 128))
    return pl.pallas_call(
        _ada_kernel,
        grid=(L, N // tn),
        in_specs=[pl.BlockSpec((G, D), lambda l, j: (0, 0)),
                  pl.BlockSpec((None, D, tn), lambda l, j: (l, 0, j)),
                  pl.BlockSpec((None, 1, tn), lambda l, j: (l, 0, j))],
        out_specs=pl.BlockSpec((None, G, tn), lambda l, j: (l, 0, j)),
        out_shape=jax.ShapeDtypeStruct((L, G, N), f32),
        compiler_params=_cparams("parallel", "parallel"),
        name="ada_mod",
    )(m_pad, w_ada, b_ada.reshape(L, 1, N))


def _rms_mod(x, g, sc=None, sh=None):
    y = x * lax.rsqrt(jnp.mean(x * x, axis=-1, keepdims=True) + EPS) * g
    if sc is not None:
        y = y * (1.0 + sc) + sh
    return y


def _norm_kernel(x_ref, g_ref, o_ref):
    o_ref[...] = _rms_mod(x_ref[...], g_ref[...]).astype(o_ref.dtype)


def final_norm(x, g):
    M, D = x.shape
    tm = _row_tile((256, 128))
    row = pl.BlockSpec((tm, D), lambda i: (i, 0))
    return pl.pallas_call(
        _norm_kernel,
        grid=(M // tm,),
        in_specs=[row, pl.BlockSpec((1, D), lambda i: (0, 0))],
        out_specs=row,
        out_shape=jax.ShapeDtypeStruct((M, D), f32),
        compiler_params=_cparams("parallel"),
        name="final_norm",
    )(x, g.reshape(1, D))


def _norm_router_kernel(x_ref, g_ref, sc_ref, sh_ref, wr_ref, h_ref, lg_ref):
    h = _rms_mod(x_ref[...], g_ref[...], sc_ref[...], sh_ref[...])
    h_ref[...] = h
    lg_ref[...] = jnp.dot(h, wr_ref[...], preferred_element_type=f32, precision=HIGHEST)


def norm_router(x, g, sc, sh, w_router_pad):
    M, D = x.shape
    tm = _row_tile((256, 128))
    row = pl.BlockSpec((tm, D), lambda i: (i, 0))
    grp = pl.BlockSpec((None, 1, D), lambda i: (_group_of_tile(i, tm), 0, 0))
    return pl.pallas_call(
        _norm_router_kernel,
        grid=(M // tm,),
        in_specs=[row, pl.BlockSpec((1, D), lambda i: (0, 0)), grp, grp,
                  pl.BlockSpec((D, LANE), lambda i: (0, 0))],
        out_specs=[row, pl.BlockSpec((tm, LANE), lambda i: (i, 0))],
        out_shape=[jax.ShapeDtypeStruct((M, D), f32), jax.ShapeDtypeStruct((M, LANE), f32)],
        compiler_params=_cparams("parallel"),
        name="norm_router",
    )(x, g.reshape(1, D), sc, sh, w_router_pad)


def _norm_specs(tm, D):
    grp = pl.BlockSpec((None, 1, D), lambda i, j: (_group_of_tile(i, tm), 0, 0))
    return [pl.BlockSpec((tm, D), lambda i, j: (i, 0)), pl.BlockSpec((1, D), lambda i, j: (0, 0)), grp, grp]


def _norm_mm_kernel(x_ref, g_ref, sc_ref, sh_ref, w_ref, o_ref, h_s):
    @pl.when(pl.program_id(1) == 0)
    def _():
        h_s[...] = _rms_mod(x_ref[...], g_ref[...], sc_ref[...], sh_ref[...]).astype(bf16)

    o_ref[...] = jnp.dot(h_s[...], w_ref[...], preferred_element_type=f32)


def norm_matmul(x, g, sc, sh, w):
    M, D = x.shape
    N = w.shape[1]
    tm = _row_tile((512, 256, 128))
    tn = _pick(N, (768, 1024, 512, 256, 128))
    return pl.pallas_call(
        _norm_mm_kernel,
        grid=(M // tm, N // tn),
        in_specs=_norm_specs(tm, D) + [pl.BlockSpec((D, tn), lambda i, j: (0, j))],
        out_specs=pl.BlockSpec((tm, tn), lambda i, j: (i, j)),
        out_shape=jax.ShapeDtypeStruct((M, N), f32),
        scratch_shapes=[pltpu.VMEM((tm, D), bf16)],
        compiler_params=_cparams("parallel", "arbitrary"),
        name="in_proj",
    )(x, g.reshape(1, D), sc, sh, w)


def _norm_glu_kernel(x_ref, g_ref, sc_ref, sh_ref, wg_ref, wu_ref, o_ref, h_s):
    @pl.when(pl.program_id(1) == 0)
    def _():
        h_s[...] = _rms_mod(x_ref[...], g_ref[...], sc_ref[...], sh_ref[...]).astype(bf16)

    h = h_s[...]
    a = jnp.dot(h, wg_ref[...], preferred_element_type=f32)
    u = jnp.dot(h, wu_ref[...], preferred_element_type=f32)
    o_ref[...] = (_silu(a) * u).astype(o_ref.dtype)


def norm_glu_up(x, g, sc, sh, wg, wu):
    M, D = x.shape
    N = wg.shape[1]
    tm = _row_tile((512, 256, 128))
    tn = _pick(N, (512, 256, 128))
    wspec = pl.BlockSpec((D, tn), lambda i, j: (0, j))
    return pl.pallas_call(
        _norm_glu_kernel,
        grid=(M // tm, N // tn),
        in_specs=_norm_specs(tm, D) + [wspec, wspec],
        out_specs=pl.BlockSpec((tm, tn), lambda i, j: (i, j)),
        out_shape=jax.ShapeDtypeStruct((M, N), bf16),
        scratch_shapes=[pltpu.VMEM((tm, D), bf16)],
        compiler_params=_cparams("parallel", "arbitrary"),
        name="glu_up",
    )(x, g.reshape(1, D), sc, sh, wg, wu)


def _mm_res_kernel(a_ref, w_ref, x_ref, g_ref, o_ref):
    y = jnp.dot(a_ref[...], w_ref[...], preferred_element_type=f32)
    o_ref[...] = x_ref[...] + g_ref[...] * y


def matmul_residual(a, w, x, gate, name="matmul_residual"):
    M, K = a.shape
    N = w.shape[1]
    tm = _row_tile((512, 256, 128))
    tn = _pick(N, (1024, 512, 256, 128))
    return pl.pallas_call(
        _mm_res_kernel,
        grid=(N // tn, M // tm),
        in_specs=[pl.BlockSpec((tm, K), lambda j, i: (i, 0)),
                  pl.BlockSpec((K, tn), lambda j, i: (0, j)),
                  pl.BlockSpec((tm, tn), lambda j, i: (i, j)),
                  pl.BlockSpec((None, 1, tn), lambda j, i: (_group_of_tile(i, tm), 0, j))],
        out_specs=pl.BlockSpec((tm, tn), lambda j, i: (i, j)),
        out_shape=jax.ShapeDtypeStruct((M, N), f32),
        compiler_params=_cparams("parallel", "parallel"),
        name=name,
    )(a, w, x, gate)


def _merge_kernel(gc_ref, ya_ref, ys_ref, yh_ref, wga_ref, wgs_ref, wgh_ref, wa_ref, ws_ref, wh_ref, o_ref):
    gc = gc_ref[...].astype(bf16)

    def branch(y_ref, wg_ref, w_ref):
        gate = _sigmoid(jnp.dot(gc, wg_ref[...], preferred_element_type=f32))
        return gate * jnp.dot(y_ref[...], w_ref[...], preferred_element_type=f32)

    o_ref[...] = (branch(ya_ref, wga_ref, wa_ref) + branch(ys_ref, wgs_ref, ws_ref)
                  + branch(yh_ref, wgh_ref, wh_ref)).astype(o_ref.dtype)


def merge_branches(proj, gc_col, y_att, y_ssd, y_hgrn, w_gate, w_att, w_ssd, w_hgrn):
    M = proj.shape[0]
    R = w_gate.shape[0]
    D = w_att.shape[1]
    tm = _row_tile((512, 256, 128))
    tn = _pick(D, (1024, 512, 256, 128))
    nj = D // tn

    def rows(w, col=0):
        return pl.BlockSpec((tm, w), lambda j, i: (i, col))

    def gate_cols(b):
        return pl.BlockSpec((R, tn), lambda j, i: (0, j + b * nj))

    def cols(k):
        return pl.BlockSpec((k, tn), lambda j, i: (0, j))

    return pl.pallas_call(
        _merge_kernel,
        grid=(nj, M // tm),
        in_specs=[rows(R, gc_col), rows(y_att.shape[1]), rows(y_ssd.shape[1]), rows(y_hgrn.shape[1]),
                  gate_cols(0), gate_cols(1), gate_cols(2),
                  cols(w_att.shape[0]), cols(w_ssd.shape[0]), cols(w_hgrn.shape[0])],
        out_specs=pl.BlockSpec((tm, tn), lambda j, i: (i, j)),
        out_shape=jax.ShapeDtypeStruct((M, D), bf16),
        compiler_params=_cparams("parallel", "parallel"),
        name="merge_branches",
    )(proj, y_att, y_ssd, y_hgrn, w_gate, w_gate, w_gate, w_att, w_ssd, w_hgrn)


def _softmax_parts(parts, sink):
    m = sink
    for s in parts:
        m = jnp.maximum(m, jnp.max(s, axis=-1, keepdims=True))
    ps = [jnp.exp(s - m) for s in parts]
    den = jnp.exp(sink - m)
    for p in ps:
        den = den + jnp.sum(p, axis=-1, keepdims=True)
    return ps, 1.0 / den


def _attend_group(sink_ref, kv, q_heads, pieces, o_ref):
    g = len(q_heads)
    outs = []
    for gi, q in enumerate(q_heads):
        scores = []
        for k, _, bias in pieces:
            s = lax.dot_general(q, k, (((1,), (1,)), ((), ())), preferred_element_type=f32)
            scores.append(s if bias is None else s + bias)
        ps, inv = _softmax_parts(scores, sink_ref[kv * g + gi])
        o = jnp.dot(ps[0].astype(bf16), pieces[0][1], preferred_element_type=f32)
        for p, (_, v, _) in zip(ps[1:], pieces[1:]):
            o = o + jnp.dot(p.astype(bf16), v, preferred_element_type=f32)
        outs.append(o * inv)
    per_lane = LANE // HEAD_DIM
    for pi in range(g // per_lane):
        tile = jnp.concatenate(outs[pi * per_lane:(pi + 1) * per_lane], axis=1)
        c0 = (kv * g + pi * per_lane) * HEAD_DIM
        o_ref[:, c0:c0 + LANE] = tile.astype(o_ref.dtype)


def _attn_ctx_kernel(sink_ref, q_ref, k_ref, v_ref, o_ref):
    g = N_HEADS // N_KV
    scale = HEAD_DIM ** -0.5
    for kv in range(N_KV):
        sl = slice(kv * HEAD_DIM, (kv + 1) * HEAD_DIM)
        k = k_ref[:, sl].astype(bf16)
        v = v_ref[:, sl].astype(bf16)
        qs = [(q_ref[:, (kv * g + gi) * HEAD_DIM:(kv * g + gi + 1) * HEAD_DIM] * scale).astype(bf16)
              for gi in range(g)]
        _attend_group(sink_ref, kv, qs, [(k, v, None)], o_ref)


def attention_context(proj, q_col, k_col, sink):
    att_w, kv_w = N_HEADS * HEAD_DIM, N_KV * HEAD_DIM
    return pl.pallas_call(
        _attn_ctx_kernel,
        grid=(BATCH,),
        in_specs=[pl.BlockSpec(memory_space=pltpu.SMEM),
                  pl.BlockSpec((SEQ, att_w), lambda b: (b, q_col)),
                  pl.BlockSpec((SEQ, kv_w), lambda b: (b, k_col)),
                  pl.BlockSpec((SEQ, kv_w), lambda b: (b, k_col + 1))],
        out_specs=pl.BlockSpec((SEQ, att_w), lambda b: (b, 0)),
        out_shape=jax.ShapeDtypeStruct((BATCH * SEQ, att_w), bf16),
        compiler_params=_cparams("parallel"),
        name="attention_context",
    )(sink, proj, proj, proj)


def _rope_kernel(n_q, n_k, q_ref, kv_ref, cos_ref, sin_ref, qo_ref, kvo_ref):
    cos = cos_ref[...]
    sin = sin_ref[...]
    quarter = HEAD_DIM // 4
    lane = lax.broadcasted_iota(jnp.int32, cos.shape, 1)
    first = (lane % (2 * quarter)) < quarter

    def rot(x):
        swapped = jnp.where(first, pltpu.roll(x, LANE - quarter, axis=1), pltpu.roll(x, quarter, axis=1))
        return x * cos + swapped * sin

    scale = HEAD_DIM ** -0.5
    for j in range(n_q):
        sl = slice(j * LANE, (j + 1) * LANE)
        qo_ref[:, sl] = (rot(q_ref[:, sl]) * scale).astype(qo_ref.dtype)
    for j in range(n_k):
        sl = slice(j * LANE, (j + 1) * LANE)
        kvo_ref[:, sl] = rot(kv_ref[:, sl]).astype(kvo_ref.dtype)
    for j in range(n_k, 2 * n_k):
        sl = slice(j * LANE, (j + 1) * LANE)
        kvo_ref[:, sl] = kv_ref[:, sl].astype(kvo_ref.dtype)


def rope_latent(proj, q_col, kv_col, cos, sin, row0):
    att_w, kv2 = N_HEADS * HEAD_DIM, 2 * N_KV * HEAD_DIM
    nl = DEC_BATCH * DEC_SEQ
    tm = _pick(DEC_SEQ, (512, 256, 128))
    off = row0 // tm
    per_seq = DEC_SEQ // tm
    tab = pl.BlockSpec((tm, LANE), lambda i: (i % per_seq, 0))
    return pl.pallas_call(
        functools.partial(_rope_kernel, att_w // LANE, kv2 // 2 // LANE),
        grid=(nl // tm,),
        in_specs=[pl.BlockSpec((tm, att_w), lambda i: (i + off, q_col)),
                  pl.BlockSpec((tm, kv2), lambda i: (i + off, kv_col)), tab, tab],
        out_specs=[pl.BlockSpec((tm, att_w), lambda i: (i, 0)), pl.BlockSpec((tm, kv2), lambda i: (i, 0))],
        out_shape=[jax.ShapeDtypeStruct((nl, att_w), bf16), jax.ShapeDtypeStruct((nl, kv2), bf16)],
        compiler_params=_cparams("parallel"),
        name="rope_latent",
    )(proj, proj, cos, sin)


def _attn_lat_kernel(nb, sink_ref, q_ref, kvp_ref, kvc_ref, kvn_ref, kc_ref, vc_ref, o_ref):
    n = pl.program_id(1)
    g = N_HEADS // N_KV
    kv_w = N_KV * HEAD_DIM
    qi = lax.broadcasted_iota(jnp.int32, (BLOCK, 3 * BLOCK), 0)
    kj = lax.broadcasted_iota(jnp.int32, (BLOCK, 3 * BLOCK), 1)
    rel = kj - BLOCK - qi
    valid = (jnp.abs(rel) <= WINDOW)
    valid = valid & ((kj >= BLOCK) | (n > 0)) & ((kj < 2 * BLOCK) | (n < nb - 1))
    bias = jnp.where(valid, 0.0, -1e30).astype(f32)
    kvw = jnp.concatenate([kvp_ref[...], kvc_ref[...], kvn_ref[...]], axis=0)
    for kv in range(N_KV):
        sl = slice(kv * HEAD_DIM, (kv + 1) * HEAD_DIM)
        k_loc = kvw[:, sl]
        v_loc = kvw[:, kv_w + kv * HEAD_DIM: kv_w + (kv + 1) * HEAD_DIM]
        qs = [q_ref[:, (kv * g + gi) * HEAD_DIM:(kv * g + gi + 1) * HEAD_DIM] for gi in range(g)]
        _attend_group(sink_ref, kv, qs, [(kc_ref[:, sl], vc_ref[:, sl], None), (k_loc, v_loc, bias)], o_ref)


def attention_latent(q_rot, kv_rot, k_ctx, v_ctx, sink):
    att_w, kv2 = q_rot.shape[1], kv_rot.shape[1]
    nb = DEC_SEQ // BLOCK
    past = k_ctx.shape[1]
    kvspec = lambda f: pl.BlockSpec((BLOCK, kv2), lambda b, n: (b * nb + f(n), 0))
    cspec = pl.BlockSpec((None, past, kv2 // 2), lambda b, n: (b, 0, 0))
    return pl.pallas_call(
        functools.partial(_attn_lat_kernel, nb),
        grid=(DEC_BATCH, nb),
        in_specs=[pl.BlockSpec(memory_space=pltpu.SMEM),
                  pl.BlockSpec((BLOCK, att_w), lambda b, n: (b * nb + n, 0)),
                  kvspec(lambda n: jnp.maximum(n - 1, 0)), kvspec(lambda n: n),
                  kvspec(lambda n: jnp.minimum(n + 1, nb - 1)), cspec, cspec],
        out_specs=pl.BlockSpec((BLOCK, att_w), lambda b, n: (b * nb + n, 0)),
        out_shape=jax.ShapeDtypeStruct((DEC_BATCH * DEC_SEQ, att_w), bf16),
        compiler_params=_cparams("parallel", "parallel"),
        name="attention_latent",
    )(sink, q_rot, kv_rot, kv_rot, kv_rot, k_ctx, v_ctx)


def _conv_silu(p_ref, c_ref, n_ref, w_ref, b_ref, has_prev, has_next):
    L = SSD_CHUNK
    lo = SUBLANE - SSD_CONV // 2
    xp = jnp.where(has_prev, p_ref[...], 0.0)
    xn = jnp.where(has_next, n_ref[...], 0.0)
    ext = jnp.concatenate([xp, c_ref[...], xn], axis=0)
    acc = b_ref[...] + ext[lo: lo + L] * w_ref[0:1, :]
    for k in range(1, SSD_CONV):
        acc = acc + ext[lo + k: lo + k + L] * w_ref[k:k + 1, :]
    return _silu(acc)


def _ssd_kernel(reverse, final, nc, *refs):
    (xp_ref, xc_ref, xn_ref, bp_ref, bc_ref, bn_ref, dt_ref, cwx_ref, cbx_ref, cwb_ref, cbb_ref,
     dtb_ref, a_ref, ex_ref, h0_ref) = refs[:15]
    if final:
        dskip_ref, z_ref, yprev_ref, norm_ref, y_ref, hfin_ref, st_ref, ybuf_ref = refs[15:]
    else:
        y_ref, hfin_ref, st_ref = refs[15:]
        ybuf_ref = y_ref
    L = SSD_CHUNK
    heads = SSD_W // SSD_HEAD_DIM
    per_lane = LANE // SSD_HEAD_DIM
    n_pairs = heads // per_lane
    pairs_per_group = n_pairs // SSD_GROUPS
    c = pl.program_id(1)
    cc = (nc - 1 - c) if reverse else c

    @pl.when(c == 0)
    def _():
        st_ref[...] = h0_ref[...]

    xs = _conv_silu(xp_ref, xc_ref, xn_ref, cwx_ref, cbx_ref, cc > 0, cc < nc - 1)
    bcm = _conv_silu(bp_ref, bc_ref, bn_ref, cwb_ref, cbb_ref, cc > 0, cc < nc - 1)

    raw = dt_ref[...] + dtb_ref[...]
    dt = jnp.maximum(raw, 0.0) + jnp.log(1.0 + jnp.exp(-jnp.abs(raw)))
    ri = lax.broadcasted_iota(jnp.int32, (L, L), 0)
    ci = lax.broadcasted_iota(jnp.int32, (L, L), 1)
    causal = (ci >= ri) if reverse else (ci <= ri)
    tri = jnp.where(causal, 1.0, 0.0).astype(f32)
    acum = jnp.dot(tri, dt * a_ref[...], preferred_element_type=f32, precision=HIGHEST)
    acum_t = acum.T
    wide = jnp.dot(jnp.concatenate([acum, dt], axis=0), ex_ref[...], preferred_element_type=f32,
                   precision=HIGHEST)
    acum_x, dt_x = wide[:L], wide[L:]
    last = 0 if reverse else L - 1
    lane0 = heads if reverse else 0
    tot_x = acum_x[last:last + 1, :]
    xdt = xs * dt_x
    xdt_end = (xdt * jnp.exp(tot_x - acum_x)).astype(bf16)
    in_scale = jnp.exp(acum_x)
    st_scale = jnp.exp(tot_x)
    low_half = lax.broadcasted_iota(jnp.int32, (L, LANE), 1) < SSD_HEAD_DIM

    for g in range(SSD_GROUPS):
        bm_f = bcm[:, g * SSD_STATE:(g + 1) * SSD_STATE]
        bm = bm_f.astype(bf16)
        bm_t = bm_f.T.astype(bf16)
        cm_off = SSD_GROUPS * SSD_STATE
        cm = bcm[:, cm_off + g * SSD_STATE: cm_off + (g + 1) * SSD_STATE].astype(bf16)
        cb = lax.dot_general(cm, bm, (((1,), (1,)), ((), ())), preferred_element_type=f32)
        cb = jnp.where(causal, cb, 0.0)
        for pp in range(pairs_per_group):
            pr = g * pairs_per_group + pp
            cols = slice(pr * LANE, (pr + 1) * LANE)
            ws = []
            for hh in range(per_lane):
                ln = lane0 + pr * per_lane + hh
                seg = acum[:, ln:ln + 1] - acum_t[ln:ln + 1, :]
                ws.append((cb * jnp.exp(jnp.minimum(seg, 0.0))).astype(bf16))
            x2 = xdt[:, cols]
            xbd = jnp.concatenate([jnp.where(low_half, x2, 0.0).astype(bf16),
                                   jnp.where(low_half, 0.0, x2).astype(bf16)], axis=0)
            y = jnp.dot(jnp.concatenate(ws, axis=1), xbd, preferred_element_type=f32)
            st = st_ref[pr]
            y = y + jnp.dot(cm, st.astype(bf16), preferred_element_type=f32) * in_scale[:, cols]
            st_ref[pr] = st * st_scale[:, cols] + jnp.dot(bm_t, xdt_end[:, cols], preferred_element_type=f32)
            ybuf_ref[:, cols] = y

    @pl.when(c == nc - 1)
    def _():
        hfin_ref[...] = st_ref[...]

    if final:
        yt = (ybuf_ref[...] + yprev_ref[...] + dskip_ref[...] * xs) * _silu(z_ref[...])
        yn = yt * lax.rsqrt(jnp.mean(yt * yt, axis=-1, keepdims=True) + EPS) * norm_ref[...]
        y_ref[...] = yn.astype(y_ref.dtype)


def ssd_state_to_lanes(h):
    n, heads, P, N = h.shape
    per_lane = LANE // P
    return h.reshape(n, heads // per_lane, per_lane, P, N).transpose(0, 1, 4, 2, 3).reshape(
        n, heads // per_lane, N, per_lane * P)


def ssd_state_from_lanes(h):
    n, pairs, N, w = h.shape
    per_lane = w // SSD_HEAD_DIM
    return h.reshape(n, pairs, N, per_lane, SSD_HEAD_DIM).transpose(0, 1, 3, 4, 2).reshape(
        n, pairs * per_lane, SSD_HEAD_DIM, N)


def ssd_pass(proj, cols, prm, h0, n_seq, T, row0, reverse, y_prev=None):
    final = y_prev is not None
    L = SSD_CHUNK
    nc = T // L
    bc_w = 2 * SSD_GROUPS * SSD_STATE
    heads = SSD_W // SSD_HEAD_DIM
    assert LANE // SSD_HEAD_DIM == 2 and (heads // 2) % SSD_GROUPS == 0
    hb = L // SUBLANE
    c0 = row0 // L
    last_blk = (proj.shape[0] // SUBLANE) - 1

    def chunk(c):
        return (nc - 1 - c) if reverse else c

    def cur(w, col):
        return pl.BlockSpec((L, w), lambda s, c: (c0 + s * nc + chunk(c), col))

    def prev(w, col):
        return pl.BlockSpec((SUBLANE, w), lambda s, c: (jnp.maximum((c0 + s * nc + chunk(c)) * hb - 1, 0), col))

    def nxt(w, col):
        return pl.BlockSpec((SUBLANE, w),
                            lambda s, c: (jnp.minimum((c0 + s * nc + chunk(c) + 1) * hb, last_blk), col))

    out_rows = pl.BlockSpec((L, SSD_W), lambda s, c: (s * nc + chunk(c), 0))
    full = lambda a: pl.BlockSpec(a.shape, lambda s, c: (0,) * a.ndim)
    st_spec = pl.BlockSpec((None,) + h0.shape[1:], lambda s, c: (s, 0, 0, 0))
    d = 1 if reverse else 0
    consts = [prm["cw_x"], prm["cb_x"], prm["cw_bc"], prm["cb_bc"], prm["dt_bias"], prm["a_lane"], prm["expand"][d]]
    in_specs = ([prev(SSD_W, cols["xs"]), cur(SSD_W, cols["xs"]), nxt(SSD_W, cols["xs"]),
                 prev(bc_w, cols["bc"]), cur(bc_w, cols["bc"]), nxt(bc_w, cols["bc"]), cur(LANE, cols["dt"])]
                + [full(a) for a in consts] + [st_spec])
    args = [proj] * 7 + consts + [h0]
    scratch = [pltpu.VMEM(h0.shape[1:], f32)]
    if final:
        in_specs += [full(prm["dskip"]), cur(SSD_W, cols["z"]), out_rows, full(prm["norm"])]
        args += [prm["dskip"], proj, y_prev, prm["norm"]]
        scratch.append(pltpu.VMEM((L, SSD_W), f32))
    return pl.pallas_call(
        functools.partial(_ssd_kernel, reverse, final, nc),
        grid=(n_seq, nc),
        in_specs=in_specs,
        out_specs=[out_rows, st_spec],
        out_shape=[jax.ShapeDtypeStruct((n_seq * T, SSD_W), bf16 if final else f32),
                   jax.ShapeDtypeStruct(h0.shape, f32)],
        scratch_shapes=scratch,
        compiler_params=_cparams("parallel", "arbitrary"),
        name="ssd_final" if final else "ssd_first",
    )(*args)


def _hgrn_kernel(reverse, final, nblk, *refs):
    hq_ref, hf_ref, hi_ref, lb_ref, s0_ref = refs[:5]
    if final:
        hg_ref, oprev_ref, norm_ref, y_ref, sfin_ref, st_ref, q_s, k_s, bc_s, f_s, o_s = refs[5:]
    else:
        y_ref, sfin_ref, st_ref, q_s, k_s, bc_s, f_s = refs[5:]
        o_s = y_ref
    R = HGRN_ROWS
    C = HGRN_CHUNK
    nsub = R // C
    heads = HGRN_W // HGRN_DK
    DK = HGRN_DK
    b = pl.program_id(1)

    @pl.when(b == 0)
    def _():
        for h in range(heads):
            st_ref[h] = s0_ref[h].T

    r = hf_ref[...]
    lb = lb_ref[...]
    e = jnp.exp(-jnp.abs(r))
    inv = 1.0 / (1.0 + e)
    sig_pos = jnp.where(r >= 0, inv, e * inv)
    sig_neg = jnp.where(r >= 0, e * inv, inv)
    f = lb + (1.0 - lb) * sig_pos
    f_s[...] = f
    logf = jnp.log(f)
    k_s[...] = (1.0 - lb) * sig_neg
    q_s[...] = _silu(hq_ref[...])
    ri = lax.broadcasted_iota(jnp.int32, (R, R), 0)
    ci = lax.broadcasted_iota(jnp.int32, (R, R), 1)
    same = (ri // C) == (ci // C)
    tri = jnp.where(same & ((ci >= ri) if reverse else (ci <= ri)), 1.0, 0.0).astype(f32)
    bc_s[...] = jnp.dot(tri, logf, preferred_element_type=f32, precision=HIGHEST)

    ii = lax.broadcasted_iota(jnp.int32, (SUBLANE, DK), 0)
    last = 0 if reverse else C - 1

    def sub_chunk(t, carry):
        sc = (nsub - 1 - t) if reverse else t
        r0 = pl.multiple_of(sc * C, C)
        rows = pl.ds(r0, C)
        for h in range(heads):
            hs = slice(h * DK, (h + 1) * DK)
            q = q_s[rows, hs]
            k = k_s[rows, hs]
            bc = bc_s[rows, hs]
            v = hi_ref[rows, hs]
            f = f_s[rows, hs]
            st = st_ref[h]
            o = lax.dot_general((q * jnp.exp(bc)).astype(bf16), st.astype(bf16), (((1,), (1,)), ((), ())),
                                preferred_element_type=f32)
            groups = [slice(gi * SUBLANE, (gi + 1) * SUBLANE) for gi in range(C // SUBLANE)]
            og = [o[s] for s in groups]
            qd = [None] * len(groups)
            for j in (range(C) if reverse else range(C - 1, -1, -1)):
                gj, jj = divmod(j, SUBLANE)
                step = j - 1 if reverse else j + 1
                for gi in (range(gj + 1) if reverse else range(gj, len(groups))):
                    prev = None if qd[gi] is None else qd[gi] * f[step:step + 1, :]
                    if gi == gj:
                        qd[gi] = jnp.where(ii == jj, q[groups[gi]], 0.0 if prev is None else prev)
                    else:
                        qd[gi] = prev
                    a = jnp.sum(qd[gi] * k[j:j + 1, :], axis=-1, keepdims=True)
                    og[gi] = og[gi] + a * v[j:j + 1, :]
            o_s[rows, hs] = jnp.concatenate(og, axis=0)
            bl = bc[last:last + 1, :]
            kd = (k * jnp.exp(bl - bc)).astype(bf16)
            upd = lax.dot_general(v.astype(bf16), kd, (((0,), (0,)), ((), ())), preferred_element_type=f32)
            st_ref[h] = st * jnp.exp(bl) + upd
        return carry

    lax.fori_loop(0, nsub, sub_chunk, 0)

    @pl.when(b == nblk - 1)
    def _():
        for h in range(heads):
            sfin_ref[h] = st_ref[h].T

    if final:
        ot = o_s[...] + oprev_ref[...]
        on = ot * lax.rsqrt(jnp.mean(ot * ot, axis=-1, keepdims=True) + EPS) * norm_ref[...]
        y_ref[...] = (on * _silu(hg_ref[...])).astype(y_ref.dtype)


def hgrn_pass(ph, f_col, lb, s0, n_seq, T, row0, reverse, o_prev=None, norm_g=None):
    final = o_prev is not None
    R = HGRN_ROWS
    nblk = T // R
    heads = HGRN_W // HGRN_DK
    b0 = row0 // R

    def blk(b):
        return (nblk - 1 - b) if reverse else b

    def col(j):
        return pl.BlockSpec((R, HGRN_W), lambda s, b: (b0 + s * nblk + blk(b), j))

    out_rows = pl.BlockSpec((R, HGRN_W), lambda s, b: (s * nblk + blk(b), 0))
    full = lambda a: pl.BlockSpec(a.shape, lambda s, b: (0,) * a.ndim)
    st_spec = pl.BlockSpec((None, heads, HGRN_DK, HGRN_DV), lambda s, b: (s, 0, 0, 0))
    in_specs = [col(0), col(f_col), col(3), full(lb), st_spec]
    args = [ph, ph, ph, lb, s0]
    scratch = [pltpu.VMEM((heads, HGRN_DV, HGRN_DK), f32)] + [pltpu.VMEM((R, HGRN_W), f32)] * 4
    if final:
        in_specs += [col(4), out_rows, full(norm_g)]
        args += [ph, o_prev, norm_g]
        scratch.append(pltpu.VMEM((R, HGRN_W), f32))
    return pl.pallas_call(
        functools.partial(_hgrn_kernel, reverse, final, nblk),
        grid=(n_seq, nblk),
        in_specs=in_specs,
        out_specs=[out_rows, st_spec],
        out_shape=[jax.ShapeDtypeStruct((n_seq * T, HGRN_W), bf16 if final else f32),
                   jax.ShapeDtypeStruct((n_seq, heads, HGRN_DK, HGRN_DV), f32)],
        scratch_shapes=scratch,
        compiler_params=_cparams("parallel", "arbitrary"),
        name="hgrn_final" if final else "hgrn_first",
    )(*args)


def _row_copy(src_hbm, dst, sem, src_row, dst_row):
    return pltpu.make_async_copy(src_hbm.at[pl.ds(src_row, 1), :], dst.at[pl.ds(dst_row, 1), :], sem)


def _moe_gather_kernel(idx_ref, h_hbm, o_ref, buf, sem):
    n = buf.shape[0]

    def start(r, c):
        _row_copy(h_hbm, buf, sem, idx_ref[0, 0, r], r).start()
        return c

    def wait(r, c):
        _row_copy(h_hbm, buf, sem, 0, r).wait()
        return c

    lax.fori_loop(0, n, start, 0)
    lax.fori_loop(0, n, wait, 0)
    o_ref[...] = buf[...].astype(o_ref.dtype)


def moe_gather(h2, slot_tok):
    D = h2.shape[1]
    n_tiles, _, tm = slot_tok.shape
    return pl.pallas_call(
        _moe_gather_kernel,
        grid=(n_tiles,),
        in_specs=[pl.BlockSpec((1, 1, tm), lambda t: (t, 0, 0), memory_space=pltpu.SMEM),
                  pl.BlockSpec(memory_space=pl.ANY)],
        out_specs=pl.BlockSpec((tm, D), lambda t: (t, 0)),
        out_shape=jax.ShapeDtypeStruct((n_tiles * tm, D), bf16),
        scratch_shapes=[pltpu.VMEM((tm, D), h2.dtype), pltpu.SemaphoreType.DMA(())],
        compiler_params=_cparams("arbitrary"),
        name="moe_gather",
    )(slot_tok, h2)


def _moe_up_kernel(te_ref, x_ref, wg_ref, wu_ref, o_ref):
    x = x_ref[...]
    a = jnp.dot(x, wg_ref[...], preferred_element_type=f32)
    u = jnp.dot(x, wu_ref[...], preferred_element_type=f32)
    o_ref[...] = (_silu(a) * u).astype(o_ref.dtype)


def moe_up(tile_e, xs, wg, wu):
    P, D = xs.shape
    E, _, F = wg.shape
    tm = MOE_TILE
    wspec = pl.BlockSpec((None, D, F), lambda t, te: (te[t], 0, 0), pipeline_mode=pl.Buffered(1))
    return pl.pallas_call(
        _moe_up_kernel,
        grid_spec=pltpu.PrefetchScalarGridSpec(
            num_scalar_prefetch=1, grid=(P // tm,),
            in_specs=[pl.BlockSpec((tm, D), lambda t, te: (t, 0)), wspec, wspec],
            out_specs=pl.BlockSpec((tm, F), lambda t, te: (t, 0))),
        out_shape=jax.ShapeDtypeStruct((P, F), bf16),
        compiler_params=_cparams("arbitrary"),
        name="moe_up",
    )(tile_e, xs, wg, wu)


def _moe_down_kernel(te_ref, a_ref, w_ref, o_ref):
    o_ref[...] = jnp.dot(a_ref[...], w_ref[...], preferred_element_type=f32)


def moe_down(tile_e, act, wd):
    P, F = act.shape
    E, _, D = wd.shape
    tm = MOE_TILE
    return pl.pallas_call(
        _moe_down_kernel,
        grid_spec=pltpu.PrefetchScalarGridSpec(
            num_scalar_prefetch=1, grid=(P // tm,),
            in_specs=[pl.BlockSpec((tm, F), lambda t, te: (t, 0)),
                      pl.BlockSpec((None, F, D), lambda t, te: (te[t], 0, 0), pipeline_mode=pl.Buffered(1))],
            out_specs=pl.BlockSpec((tm, D), lambda t, te: (t, 0))),
        out_shape=jax.ShapeDtypeStruct((P, D), f32),
        compiler_params=_cparams("arbitrary"),
        name="moe_down",
    )(tile_e, act, wd)


def _moe_combine_kernel(pos_ref, w_ref, x_ref, g_ref, ys_hbm, o_ref, buf, sem):
    n = x_ref.shape[0]

    def start(r, c):
        for kk in range(TOP_K):
            _row_copy(ys_hbm, buf.at[kk], sem, pos_ref[0, kk, r], r).start()
        return c

    def wait(r, c):
        for kk in range(TOP_K):
            _row_copy(ys_hbm, buf.at[kk], sem, 0, r).wait()
        return c

    lax.fori_loop(0, n, start, 0)
    lax.fori_loop(0, n, wait, 0)
    w = w_ref[...]
    y = w[:, 0:1] * buf[0]
    for kk in range(1, TOP_K):
        y = y + w[:, kk:kk + 1] * buf[kk]
    o_ref[...] = x_ref[...] + g_ref[...] * y


def moe_combine(x, gate, ys, pos, top_w):
    M, D = x.shape
    tm = pos.shape[2]
    return pl.pallas_call(
        _moe_combine_kernel,
        grid=(M // tm,),
        in_specs=[pl.BlockSpec((1, TOP_K, tm), lambda i: (i, 0, 0), memory_space=pltpu.SMEM),
                  pl.BlockSpec((tm, TOP_K), lambda i: (i, 0)),
                  pl.BlockSpec((tm, D), lambda i: (i, 0)),
                  pl.BlockSpec((None, 1, D), lambda i: (_group_of_tile(i, tm), 0, 0)),
                  pl.BlockSpec(memory_space=pl.ANY)],
        out_specs=pl.BlockSpec((tm, D), lambda i: (i, 0)),
        out_shape=jax.ShapeDtypeStruct((M, D), f32),
        scratch_shapes=[pltpu.VMEM((TOP_K, tm, D), f32), pltpu.SemaphoreType.DMA(())],
        compiler_params=_cparams("arbitrary"),
        name="moe_combine",
    )(pos, top_w, x, gate, ys)


def moe_ffn(x, norm_g, sc, sh, gate, w_router, wg, wu, wd):
    M, D = x.shape
    E = w_router.shape[1]
    wr = jnp.zeros((D, LANE), f32).at[:, :E].set(w_router)
    h2, logits = norm_router(x, norm_g, sc, sh, wr)
    top_v, top_i = lax.top_k(logits[:, :E], TOP_K)
    top_w = jax.nn.softmax(top_v, axis=-1)
    nk = M * TOP_K
    e_flat = top_i.reshape(-1).astype(jnp.int32)
    order = jnp.argsort(e_flat).astype(jnp.int32)
    e_sorted = e_flat[order]
    counts = jnp.sum((e_flat[:, None] == jnp.arange(E, dtype=jnp.int32)[None, :]).astype(jnp.int32), axis=0)
    padded = (counts + MOE_TILE - 1) // MOE_TILE * MOE_TILE
    start = jnp.cumsum(counts) - counts
    pend = jnp.cumsum(padded)
    pstart = pend - padded
    pos_sorted = pstart[e_sorted] + jnp.arange(nk, dtype=jnp.int32) - start[e_sorted]
    n_tiles = -(-nk // MOE_TILE) + E
    slot_tok = jnp.zeros((n_tiles * MOE_TILE,), jnp.int32).at[pos_sorted].set(order // TOP_K)
    pos = jnp.zeros((nk,), jnp.int32).at[order].set(pos_sorted)
    tile_e = jnp.minimum(jnp.searchsorted(pend, jnp.arange(n_tiles, dtype=jnp.int32) * MOE_TILE, side='right'),
                         E - 1).astype(jnp.int32)
    xs = moe_gather(h2, slot_tok.reshape(n_tiles, 1, MOE_TILE))
    act = moe_up(tile_e, xs, wg, wu)
    ys = moe_down(tile_e, act, wd)
    tc = _row_tile((256, 128))
    pos_t = pos.reshape(M // tc, tc, TOP_K).transpose(0, 2, 1)
    return moe_combine(x, gate, ys, pos_t, top_w)


def _rope_tables():
    rows = DEC_SEQ // GRID_W
    row = jnp.repeat(jnp.arange(rows), GRID_W)
    col = jnp.tile(jnp.arange(GRID_W), rows)
    quarter = HEAD_DIM // 4
    inv = ROPE_THETA ** (-jnp.arange(quarter, dtype=f32) / quarter)
    ar = row.astype(f32)[:, None] * inv[None, :]
    ac = col.astype(f32)[:, None] * inv[None, :]
    cos = jnp.concatenate([jnp.cos(ar), jnp.cos(ar), jnp.cos(ac), jnp.cos(ac)], axis=-1)
    sin = jnp.concatenate([-jnp.sin(ar), jnp.sin(ar), -jnp.sin(ac), jnp.sin(ac)], axis=-1)
    rep = LANE // HEAD_DIM
    return jnp.tile(cos, (1, rep)), jnp.tile(sin, (1, rep))


def kernel(x_prompt, x_sample, cache_k, cache_v, state_ssd, state_hgrn, c, c_ctx, w_ada, b_ada, norm1_g, norm2_g, w_in, attn_sink, ssd_conv_w, ssd_conv_b, ssd_dt_bias, ssd_a_log, ssd_d, ssd_norm_g, hgrn_lb, hgrn_norm_g, w_gate, w_br_att, w_br_ssd, w_br_hgrn, w_out, ffn_wg, ffn_wu, ffn_wd, router_w, moe_wg, moe_wu, moe_wd, final_g):
    att_w, kv_w, ssd_heads, conv_ch, hgrn_heads, nc, nl = _dims()
    D = D_MODEL
    G = 1 + DEC_BATCH
    g_pad = -(-G // SUBLANE) * SUBLANE
    x = jnp.concatenate([x_prompt.reshape(nc, D), x_sample.reshape(nl, D)], axis=0)
    m = jnp.zeros((g_pad, D), f32).at[0].set(c_ctx).at[1:G].set(c)
    mods = ada_mod(m, w_ada, b_ada)
    sm = jax.nn.softmax(hgrn_lb.astype(f32), axis=1)
    lb_all = jnp.cumsum(sm, axis=1) - sm[:, :1]
    cos, sin = _rope_tables()

    cuts = np.cumsum([0, att_w, kv_w, kv_w, SSD_W, conv_ch, 2 * ssd_heads, 5 * HGRN_W, GATE_RANK])
    bc_w = conv_ch - SSD_W
    off, o = {}, 0
    for name, w in (("hgrn", 5 * HGRN_W), ("q", att_w), ("z", SSD_W), ("xs", SSD_W), ("bc", bc_w),
                    ("kv", 2 * kv_w), ("gc", GATE_RANK), ("dt", LANE)):
        off[name] = o
        o += w

    def blk(name, width):
        assert off[name] % width == 0
        return off[name] // width

    ssd_cols = {"xs": blk("xs", SSD_W), "bc": blk("bc", bc_w), "z": blk("z", SSD_W), "dt": blk("dt", LANE)}
    expand = np.zeros((2, LANE, SSD_W), np.float32)
    for d in range(2):
        for hh in range(ssd_heads):
            expand[d, d * ssd_heads + hh, hh * SSD_HEAD_DIM:(hh + 1) * SSD_HEAD_DIM] = 1.0
    expand = jnp.asarray(expand)
    lanes = jnp.zeros((1, LANE), f32)
    zero_ssd = jnp.zeros((BATCH, ssd_heads * SSD_HEAD_DIM // LANE, SSD_STATE, LANE), f32)
    zero_h = jnp.zeros((BATCH, hgrn_heads, HGRN_DK, HGRN_DV), f32)

    ks, vs, hs, ss = [], [], [], []
    for l in range(DEPTH):
        mod = mods[l].reshape(g_pad, 6, 1, D)
        sh1, sc1, g1, sh2, sc2, g2 = [mod[:, j] for j in range(6)]
        wl = w_in[l]
        w_proj = jnp.concatenate(
            [wl[:, cuts[6]:cuts[7]], wl[:, cuts[0]:cuts[1]], wl[:, cuts[3]:cuts[4]],
             wl[:, cuts[4]:cuts[4] + SSD_W], wl[:, cuts[4] + SSD_W:cuts[5]], wl[:, cuts[1]:cuts[3]],
             wl[:, cuts[7]:cuts[8]], wl[:, cuts[5]:cuts[6]], jnp.zeros((D, DT_PAD - 2 * ssd_heads), f32)],
            axis=1).astype(bf16)
        proj = norm_matmul(x, norm1_g[l], sc1, sh1, w_proj)

        y_att_c = attention_context(proj, blk("q", att_w), blk("kv", kv_w), attn_sink[l])
        q_rot, kv_rot = rope_latent(proj, blk("q", att_w), blk("kv", 2 * kv_w), cos, sin, nc)
        y_att_l = attention_latent(q_rot, kv_rot,
                                   cache_k[:, l].reshape(DEC_BATCH, PAST_LEN, kv_w).astype(bf16),
                                   cache_v[:, l].reshape(DEC_BATCH, PAST_LEN, kv_w).astype(bf16), attn_sink[l])
        y_att = jnp.concatenate([y_att_c, y_att_l], axis=0)
        ks.append(proj[:nc, off["kv"]:off["kv"] + kv_w].reshape(BATCH, SEQ, N_KV, HEAD_DIM))
        vs.append(proj[:nc, off["kv"] + kv_w:off["kv"] + 2 * kv_w].reshape(BATCH, SEQ, N_KV, HEAD_DIM))

        prm = {
            "cw_x": ssd_conv_w[l][:, :SSD_W], "cb_x": ssd_conv_b[l][:SSD_W].reshape(1, SSD_W),
            "cw_bc": ssd_conv_w[l][:, SSD_W:], "cb_bc": ssd_conv_b[l][SSD_W:].reshape(1, bc_w),
            "dt_bias": lanes.at[0, :2 * ssd_heads].set(ssd_dt_bias[l].reshape(-1)),
            "a_lane": lanes.at[0, :2 * ssd_heads].set(-jnp.exp(ssd_a_log[l].astype(f32)).reshape(-1)),
            "expand": expand,
            "dskip": jnp.repeat(ssd_d[l].astype(f32), SSD_HEAD_DIM).reshape(1, SSD_W),
            "norm": ssd_norm_g[l].reshape(1, SSD_W),
        }
        y_ssd, h_ssd = [], []
        for (n_seq, T, row0, h0f, h0b) in (
                (BATCH, SEQ, 0, zero_ssd, zero_ssd),
                (DEC_BATCH, DEC_SEQ, nc, ssd_state_to_lanes(state_ssd[:, l, 0]), ssd_state_to_lanes(state_ssd[:, l, 1]))):
            yf, hf = ssd_pass(proj, ssd_cols, prm, h0f, n_seq, T, row0, False)
            yb, hb = ssd_pass(proj, ssd_cols, prm, h0b, n_seq, T, row0, True, y_prev=yf)
            y_ssd.append(yb)
            h_ssd.append((hf, hb))
        hs.append(jnp.stack([ssd_state_from_lanes(t) for t in h_ssd[0]], axis=1))
        y_ssd = jnp.concatenate(y_ssd, axis=0)

        assert off["hgrn"] == 0
        lb = lb_all[:, l]
        hn = hgrn_norm_g[l].reshape(1, HGRN_W)
        y_hgrn, s_hgrn = [], []
        for (n_seq, T, row0, s0f, s0b) in ((BATCH, SEQ, 0, zero_h, zero_h),
                                           (DEC_BATCH, DEC_SEQ, nc, state_hgrn[:, l, 0], state_hgrn[:, l, 1])):
            of, sf = hgrn_pass(proj, 1, lb[0:1], s0f, n_seq, T, row0, False)
            ob, sb = hgrn_pass(proj, 2, lb[1:2], s0b, n_seq, T, row0, True, o_prev=of, norm_g=hn)
            y_hgrn.append(ob)
            s_hgrn.append((sf, sb))
        ss.append(jnp.stack(s_hgrn[0], axis=1))
        y_hgrn = jnp.concatenate(y_hgrn, axis=0)

        merged = merge_branches(proj, blk("gc", GATE_RANK), y_att, y_ssd, y_hgrn, w_gate[l].astype(bf16),
                                w_br_att[l].astype(bf16), w_br_ssd[l].astype(bf16), w_br_hgrn[l].astype(bf16))
        x = matmul_residual(merged, w_out[l].astype(bf16), x, g1, "out_proj")

        i = l // 2
        if l % 2 == 0:
            act = norm_glu_up(x, norm2_g[l], sc2, sh2, ffn_wg[i].astype(bf16), ffn_wu[i].astype(bf16))
            x = matmul_residual(act, ffn_wd[i].astype(bf16), x, g2, "ffn_down")
        else:
            x = moe_ffn(x, norm2_g[l], sc2, sh2, g2, router_w[i], moe_wg[i].astype(bf16), moe_wu[i].astype(bf16),
                        moe_wd[i].astype(bf16))

    y = final_norm(x, final_g)
    y_prompt = y[:nc].reshape(BATCH, SEQ, D)
    y_sample = y[nc:].reshape(DEC_BATCH, DEC_SEQ, D)
    return (y_prompt, y_sample, jnp.stack(ks, axis=1), jnp.stack(vs, axis=1),
            jnp.stack(hs, axis=1), jnp.stack(ss, axis=1))
```

```python
import functools

import numpy as np
import jax
import jax.numpy as jnp
from jax import lax
from jax.experimental import pallas as pl
from jax.experimental.pallas import tpu as pltpu

f32 = jnp.float32
bf16 = jnp.bfloat16

D_MODEL = 4096
BATCH = 16
SEQ = 256
DEPTH = 4
DEC_BATCH = 4
DEC_SEQ = 4096
PAST_LEN = 256
GRID_W = 64
N_HEADS = 16
N_KV = 4
HEAD_DIM = 64
WINDOW = 128
BLOCK = 128
ROPE_THETA = 10000.0
SSD_W = 1024
SSD_HEAD_DIM = 64
SSD_GROUPS = 2
SSD_STATE = 128
SSD_CONV = 5
SSD_CHUNK = 128
HGRN_W = 1024
HGRN_DK = 128
HGRN_DV = 128
HGRN_CHUNK = 16
GATE_RANK = 512
D_FF = 5632
N_EXPERTS = 8
TOP_K = 2
EXPERT_FF = 1408
EPS = 1e-6

LANE = 128
SUBLANE = 8
VMEM_LIMIT = 56 * 1024 * 1024
MOE_TILE = 512
HGRN_ROWS = 128
DT_PAD = 2 * LANE
HIGHEST = lax.Precision.HIGHEST


def _cparams(*sem):
    return pltpu.CompilerParams(dimension_semantics=sem, vmem_limit_bytes=VMEM_LIMIT)


def _pick(n, prefs):
    for p in prefs:
        if n % p == 0:
            return p
    return n


def _silu(x):
    return x * (1.0 / (1.0 + jnp.exp(-x)))


def _sigmoid(x):
    return 1.0 / (1.0 + jnp.exp(-x))


def _dims():
    att_w = N_HEADS * HEAD_DIM
    kv_w = N_KV * HEAD_DIM
    ssd_heads = SSD_W // SSD_HEAD_DIM
    conv_ch = SSD_W + 2 * SSD_GROUPS * SSD_STATE
    hgrn_heads = HGRN_W // HGRN_DK
    nc = BATCH * SEQ
    nl = DEC_BATCH * DEC_SEQ
    return att_w, kv_w, ssd_heads, conv_ch, hgrn_heads, nc, nl


def _group_of_tile(i, tm):
    nc = BATCH * SEQ
    r = i * tm
    return jnp.where(r < nc, 0, 1 + (r - nc) // DEC_SEQ)


def _row_tile(prefs=(512, 256, 128)):
    nc = BATCH * SEQ
    return _pick(int(np.gcd(nc, DEC_SEQ)), prefs)


def _ada_kernel(m_ref, w_ref, b_ref, o_ref):
    a = _silu(m_ref[...]).astype(bf16)
    o_ref[...] = jnp.dot(a, w_ref[...].astype(bf16), preferred_element_type=f32) + b_ref[...]


def ada_mod(m_pad, w_ada, b_ada):
    L, D, N = w_ada.shape
    G = m_pad.shape[0]
    tn = _pick(N, (512, 256, 128))
    return pl.pallas_call(
        _ada_kernel,
        grid=(L, N // tn),
        in_specs=[pl.BlockSpec((G, D), lambda l, j: (0, 0)),
                  pl.BlockSpec((None, D, tn), lambda l, j: (l, 0, j)),
                  pl.BlockSpec((None, 1, tn), lambda l, j: (l, 0, j))],
        out_specs=pl.BlockSpec((None, G, tn), lambda l, j: (l, 0, j)),
        out_shape=jax.ShapeDtypeStruct((L, G, N), f32),
        compiler_params=_cparams("parallel", "parallel"),
        name="ada_mod",
    )(m_pad, w_ada, b_ada.reshape(L, 1, N))


def _rms_mod(x, g, sc=None, sh=None):
    y = x * lax.rsqrt(jnp.mean(x * x, axis=-1, keepdims=True) + EPS) * g
    if sc is not None:
        y = y * (1.0 + sc) + sh
    return y


def _norm_kernel(x_ref, g_ref, o_ref):
    o_ref[...] = _rms_mod(x_ref[...], g_ref[...]).astype(o_ref.dtype)


def final_norm(x, g):
    M, D = x.shape
    tm = _row_tile((256, 128))
    row = pl.BlockSpec((tm, D), lambda i: (i, 0))
    return pl.pallas_call(
        _norm_kernel,
        grid=(M // tm,),
        in_specs=[row, pl.BlockSpec((1, D), lambda i: (0, 0))],
        out_specs=row,
        out_shape=jax.ShapeDtypeStruct((M, D), f32),
        compiler_params=_cparams("parallel"),
        name="final_norm",
    )(x, g.reshape(1, D))


def _norm_router_kernel(x_ref, g_ref, sc_ref, sh_ref, wr_ref, h_ref, lg_ref):
    h = _rms_mod(x_ref[...], g_ref[...], sc_ref[...], sh_ref[...])
    h_ref[...] = h
    lg_ref[...] = jnp.dot(h, wr_ref[...], preferred_element_type=f32, precision=HIGHEST)


def norm_router(x, g, sc, sh, w_router_pad):
    M, D = x.shape
    tm = _row_tile((256, 128))
    row = pl.BlockSpec((tm, D), lambda i: (i, 0))
    grp = pl.BlockSpec((None, 1, D), lambda i: (_group_of_tile(i, tm), 0, 0))
    return pl.pallas_call(
        _norm_router_kernel,
        grid=(M // tm,),
        in_specs=[row, pl.BlockSpec((1, D), lambda i: (0, 0)), grp, grp,
                  pl.BlockSpec((D, LANE), lambda i: (0, 0))],
        out_specs=[row, pl.BlockSpec((tm, LANE), lambda i: (i, 0))],
        out_shape=[jax.ShapeDtypeStruct((M, D), f32), jax.ShapeDtypeStruct((M, LANE), f32)],
        compiler_params=_cparams("parallel"),
        name="norm_router",
    )(x, g.reshape(1, D), sc, sh, w_router_pad)


def _norm_specs(tm, D):
    grp = pl.BlockSpec((None, 1, D), lambda i, j: (_group_of_tile(i, tm), 0, 0))
    return [pl.BlockSpec((tm, D), lambda i, j: (i, 0), pipeline_mode=pl.Buffered(1)),
            pl.BlockSpec((1, D), lambda i, j: (0, 0)), grp, grp]


def _norm_rows_to(h_s, x_ref, g_ref, sc_ref, sh_ref):
    rows = _pick(x_ref.shape[0], (256, 128))

    def chunk(c, carry):
        r = pl.ds(pl.multiple_of(c * rows, rows), rows)
        h_s[r, :] = _rms_mod(x_ref[r, :], g_ref[...], sc_ref[...], sh_ref[...]).astype(bf16)
        return carry

    lax.fori_loop(0, x_ref.shape[0] // rows, chunk, 0)


def _norm_mm_kernel(x_ref, g_ref, sc_ref, sh_ref, w_ref, o_ref, h_s):
    @pl.when(pl.program_id(1) == 0)
    def _():
        _norm_rows_to(h_s, x_ref, g_ref, sc_ref, sh_ref)

    o_ref[...] = jnp.dot(h_s[...], w_ref[...], preferred_element_type=f32)


def norm_matmul(x, g, sc, sh, w):
    M, D = x.shape
    N = w.shape[1]
    tm = _row_tile((1024, 512, 256, 128))
    tn = _pick(N, (768, 1024, 512, 256, 128))
    return pl.pallas_call(
        _norm_mm_kernel,
        grid=(M // tm, N // tn),
        in_specs=_norm_specs(tm, D) + [pl.BlockSpec((D, tn), lambda i, j: (0, j))],
        out_specs=pl.BlockSpec((tm, tn), lambda i, j: (i, j)),
        out_shape=jax.ShapeDtypeStruct((M, N), f32),
        scratch_shapes=[pltpu.VMEM((tm, D), bf16)],
        compiler_params=_cparams("parallel", "arbitrary"),
        name="in_proj",
    )(x, g.reshape(1, D), sc, sh, w)


def _norm_glu_kernel(x_ref, g_ref, sc_ref, sh_ref, wg_ref, wu_ref, o_ref, h_s):
    @pl.when(pl.program_id(1) == 0)
    def _():
        _norm_rows_to(h_s, x_ref, g_ref, sc_ref, sh_ref)

    h = h_s[...]
    a = jnp.dot(h, wg_ref[...], preferred_element_type=f32)
    u = jnp.dot(h, wu_ref[...], preferred_element_type=f32)
    o_ref[...] = (_silu(a) * u).astype(o_ref.dtype)


def norm_glu_up(x, g, sc, sh, wg, wu):
    M, D = x.shape
    N = wg.shape[1]
    tm = _row_tile((1024, 512, 256, 128))
    tn = _pick(N, (512, 256, 128))
    wspec = pl.BlockSpec((D, tn), lambda i, j: (0, j))
    return pl.pallas_call(
        _norm_glu_kernel,
        grid=(M // tm, N // tn),
        in_specs=_norm_specs(tm, D) + [wspec, wspec],
        out_specs=pl.BlockSpec((tm, tn), lambda i, j: (i, j)),
        out_shape=jax.ShapeDtypeStruct((M, N), bf16),
        scratch_shapes=[pltpu.VMEM((tm, D), bf16)],
        compiler_params=_cparams("parallel", "arbitrary"),
        name="glu_up",
    )(x, g.reshape(1, D), sc, sh, wg, wu)


def _mm_res_kernel(a_ref, w_ref, x_ref, g_ref, o_ref):
    y = jnp.dot(a_ref[...], w_ref[...], preferred_element_type=f32)
    o_ref[...] = x_ref[...] + g_ref[...] * y


def matmul_residual(a, w, x, gate, name="matmul_residual"):
    M, K = a.shape
    N = w.shape[1]
    tm = _row_tile((512, 256, 128))
    tn = _pick(N, (1024, 512, 256, 128))
    return pl.pallas_call(
        _mm_res_kernel,
        grid=(N // tn, M // tm),
        in_specs=[pl.BlockSpec((tm, K), lambda j, i: (i, 0)),
                  pl.BlockSpec((K, tn), lambda j, i: (0, j)),
                  pl.BlockSpec((tm, tn), lambda j, i: (i, j)),
                  pl.BlockSpec((None, 1, tn), lambda j, i: (_group_of_tile(i, tm), 0, j))],
        out_specs=pl.BlockSpec((tm, tn), lambda j, i: (i, j)),
        out_shape=jax.ShapeDtypeStruct((M, N), f32),
        compiler_params=_cparams("parallel", "parallel"),
        name=name,
    )(a, w, x, gate)


def _merge_kernel(gc_ref, ya_ref, ys_ref, yh_ref, wga_ref, wgs_ref, wgh_ref, wa_ref, ws_ref, wh_ref, o_ref):
    gc = gc_ref[...].astype(bf16)

    def branch(y_ref, wg_ref, w_ref):
        gate = _sigmoid(jnp.dot(gc, wg_ref[...], preferred_element_type=f32))
        return gate * jnp.dot(y_ref[...], w_ref[...], preferred_element_type=f32)

    o_ref[...] = (branch(ya_ref, wga_ref, wa_ref) + branch(ys_ref, wgs_ref, ws_ref)
                  + branch(yh_ref, wgh_ref, wh_ref)).astype(o_ref.dtype)


def merge_branches(proj, gc_col, y_att, y_ssd, y_hgrn, w_gate, w_att, w_ssd, w_hgrn):
    M = proj.shape[0]
    R = w_gate.shape[0]
    D = w_att.shape[1]
    tm = _row_tile((512, 256, 128))
    tn = _pick(D, (1024, 512, 256, 128))
    nj = D // tn

    def rows(w, col=0):
        return pl.BlockSpec((tm, w), lambda j, i: (i, col))

    def gate_cols(b):
        return pl.BlockSpec((R, tn), lambda j, i: (0, j + b * nj))

    def cols(k):
        return pl.BlockSpec((k, tn), lambda j, i: (0, j))

    return pl.pallas_call(
        _merge_kernel,
        grid=(nj, M // tm),
        in_specs=[rows(R, gc_col), rows(y_att.shape[1]), rows(y_ssd.shape[1]), rows(y_hgrn.shape[1]),
                  gate_cols(0), gate_cols(1), gate_cols(2),
                  cols(w_att.shape[0]), cols(w_ssd.shape[0]), cols(w_hgrn.shape[0])],
        out_specs=pl.BlockSpec((tm, tn), lambda j, i: (i, j)),
        out_shape=jax.ShapeDtypeStruct((M, D), bf16),
        compiler_params=_cparams("parallel", "parallel"),
        name="merge_branches",
    )(proj, y_att, y_ssd, y_hgrn, w_gate, w_gate, w_gate, w_att, w_ssd, w_hgrn)


def _softmax_parts(parts, sink):
    m = sink
    for s in parts:
        m = jnp.maximum(m, jnp.max(s, axis=-1, keepdims=True))
    ps = [jnp.exp(s - m) for s in parts]
    den = jnp.exp(sink - m)
    for p in ps:
        den = den + jnp.sum(p, axis=-1, keepdims=True)
    return ps, 1.0 / den


def _attend_group(sink_ref, kv, q_heads, pieces, o_ref):
    g = len(q_heads)
    outs = []
    for gi, q in enumerate(q_heads):
        scores = []
        for k, _, bias in pieces:
            s = lax.dot_general(q, k, (((1,), (1,)), ((), ())), preferred_element_type=f32)
            scores.append(s if bias is None else s + bias)
        ps, inv = _softmax_parts(scores, sink_ref[kv * g + gi])
        o = jnp.dot(ps[0].astype(bf16), pieces[0][1], preferred_element_type=f32)
        for p, (_, v, _) in zip(ps[1:], pieces[1:]):
            o = o + jnp.dot(p.astype(bf16), v, preferred_element_type=f32)
        outs.append(o * inv)
    per_lane = LANE // HEAD_DIM
    for pi in range(g // per_lane):
        tile = jnp.concatenate(outs[pi * per_lane:(pi + 1) * per_lane], axis=1)
        c0 = (kv * g + pi * per_lane) * HEAD_DIM
        o_ref[:, c0:c0 + LANE] = tile.astype(o_ref.dtype)


def _attn_ctx_kernel(sink_ref, q_ref, k_ref, v_ref, o_ref):
    g = N_HEADS // N_KV
    scale = HEAD_DIM ** -0.5
    for kv in range(N_KV):
        sl = slice(kv * HEAD_DIM, (kv + 1) * HEAD_DIM)
        k = k_ref[:, sl].astype(bf16)
        v = v_ref[:, sl].astype(bf16)
        qs = [(q_ref[:, (kv * g + gi) * HEAD_DIM:(kv * g + gi + 1) * HEAD_DIM] * scale).astype(bf16)
              for gi in range(g)]
        _attend_group(sink_ref, kv, qs, [(k, v, None)], o_ref)


def attention_context(proj, q_col, k_col, sink):
    att_w, kv_w = N_HEADS * HEAD_DIM, N_KV * HEAD_DIM
    return pl.pallas_call(
        _attn_ctx_kernel,
        grid=(BATCH,),
        in_specs=[pl.BlockSpec(memory_space=pltpu.SMEM),
                  pl.BlockSpec((SEQ, att_w), lambda b: (b, q_col)),
                  pl.BlockSpec((SEQ, kv_w), lambda b: (b, k_col)),
                  pl.BlockSpec((SEQ, kv_w), lambda b: (b, k_col + 1))],
        out_specs=pl.BlockSpec((SEQ, att_w), lambda b: (b, 0)),
        out_shape=jax.ShapeDtypeStruct((BATCH * SEQ, att_w), bf16),
        compiler_params=_cparams("parallel"),
        name="attention_context",
    )(sink, proj, proj, proj)


def _rope_kernel(n_q, n_k, q_ref, kv_ref, cos_ref, sin_ref, qo_ref, kvo_ref):
    cos = cos_ref[...]
    sin = sin_ref[...]
    quarter = HEAD_DIM // 4
    lane = lax.broadcasted_iota(jnp.int32, cos.shape, 1)
    first = (lane % (2 * quarter)) < quarter

    def rot(x):
        swapped = jnp.where(first, pltpu.roll(x, LANE - quarter, axis=1), pltpu.roll(x, quarter, axis=1))
        return x * cos + swapped * sin

    scale = HEAD_DIM ** -0.5
    for j in range(n_q):
        sl = slice(j * LANE, (j + 1) * LANE)
        qo_ref[:, sl] = (rot(q_ref[:, sl]) * scale).astype(qo_ref.dtype)
    for j in range(n_k):
        sl = slice(j * LANE, (j + 1) * LANE)
        kvo_ref[:, sl] = rot(kv_ref[:, sl]).astype(kvo_ref.dtype)
    for j in range(n_k, 2 * n_k):
        sl = slice(j * LANE, (j + 1) * LANE)
        kvo_ref[:, sl] = kv_ref[:, sl].astype(kvo_ref.dtype)


def rope_latent(proj, q_col, kv_col, cos, sin, row0):
    att_w, kv2 = N_HEADS * HEAD_DIM, 2 * N_KV * HEAD_DIM
    nl = DEC_BATCH * DEC_SEQ
    tm = _pick(DEC_SEQ, (512, 256, 128))
    off = row0 // tm
    per_seq = DEC_SEQ // tm
    tab = pl.BlockSpec((tm, LANE), lambda i: (i % per_seq, 0))
    return pl.pallas_call(
        functools.partial(_rope_kernel, att_w // LANE, kv2 // 2 // LANE),
        grid=(nl // tm,),
        in_specs=[pl.BlockSpec((tm, att_w), lambda i: (i + off, q_col)),
                  pl.BlockSpec((tm, kv2), lambda i: (i + off, kv_col)), tab, tab],
        out_specs=[pl.BlockSpec((tm, att_w), lambda i: (i, 0)), pl.BlockSpec((tm, kv2), lambda i: (i, 0))],
        out_shape=[jax.ShapeDtypeStruct((nl, att_w), bf16), jax.ShapeDtypeStruct((nl, kv2), bf16)],
        compiler_params=_cparams("parallel"),
        name="rope_latent",
    )(proj, proj, cos, sin)


def _attn_lat_kernel(nb, sink_ref, q_ref, kvp_ref, kvc_ref, kvn_ref, kc_ref, vc_ref, o_ref):
    n = pl.program_id(1)
    g = N_HEADS // N_KV
    kv_w = N_KV * HEAD_DIM
    qi = lax.broadcasted_iota(jnp.int32, (BLOCK, 3 * BLOCK), 0)
    kj = lax.broadcasted_iota(jnp.int32, (BLOCK, 3 * BLOCK), 1)
    rel = kj - BLOCK - qi
    valid = (jnp.abs(rel) <= WINDOW)
    valid = valid & ((kj >= BLOCK) | (n > 0)) & ((kj < 2 * BLOCK) | (n < nb - 1))
    bias = jnp.where(valid, 0.0, -1e30).astype(f32)
    kvw = jnp.concatenate([kvp_ref[...], kvc_ref[...], kvn_ref[...]], axis=0)
    for kv in range(N_KV):
        sl = slice(kv * HEAD_DIM, (kv + 1) * HEAD_DIM)
        k_loc = kvw[:, sl]
        v_loc = kvw[:, kv_w + kv * HEAD_DIM: kv_w + (kv + 1) * HEAD_DIM]
        qs = [q_ref[:, (kv * g + gi) * HEAD_DIM:(kv * g + gi + 1) * HEAD_DIM] for gi in range(g)]
        _attend_group(sink_ref, kv, qs, [(kc_ref[:, sl], vc_ref[:, sl], None), (k_loc, v_loc, bias)], o_ref)


def attention_latent(q_rot, kv_rot, k_ctx, v_ctx, sink):
    att_w, kv2 = q_rot.shape[1], kv_rot.shape[1]
    nb = DEC_SEQ // BLOCK
    past = k_ctx.shape[1]
    kvspec = lambda f: pl.BlockSpec((BLOCK, kv2), lambda b, n: (b * nb + f(n), 0))
    cspec = pl.BlockSpec((None, past, kv2 // 2), lambda b, n: (b, 0, 0))
    return pl.pallas_call(
        functools.partial(_attn_lat_kernel, nb),
        grid=(DEC_BATCH, nb),
        in_specs=[pl.BlockSpec(memory_space=pltpu.SMEM),
                  pl.BlockSpec((BLOCK, att_w), lambda b, n: (b * nb + n, 0)),
                  kvspec(lambda n: jnp.maximum(n - 1, 0)), kvspec(lambda n: n),
                  kvspec(lambda n: jnp.minimum(n + 1, nb - 1)), cspec, cspec],
        out_specs=pl.BlockSpec((BLOCK, att_w), lambda b, n: (b * nb + n, 0)),
        out_shape=jax.ShapeDtypeStruct((DEC_BATCH * DEC_SEQ, att_w), bf16),
        compiler_params=_cparams("parallel", "parallel"),
        name="attention_latent",
    )(sink, q_rot, kv_rot, kv_rot, kv_rot, k_ctx, v_ctx)


def _conv_silu(p_ref, c_ref, n_ref, w_ref, b_ref, has_prev, has_next):
    L = SSD_CHUNK
    lo = SUBLANE - SSD_CONV // 2
    xp = jnp.where(has_prev, p_ref[...], 0.0)
    xn = jnp.where(has_next, n_ref[...], 0.0)
    ext = jnp.concatenate([xp, c_ref[...], xn], axis=0)
    acc = b_ref[...] + ext[lo: lo + L] * w_ref[0:1, :]
    for k in range(1, SSD_CONV):
        acc = acc + ext[lo + k: lo + k + L] * w_ref[k:k + 1, :]
    return _silu(acc)


def _ssd_kernel(reverse, final, nc, *refs):
    (xp_ref, xc_ref, xn_ref, bp_ref, bc_ref, bn_ref, dt_ref, cwx_ref, cbx_ref, cwb_ref, cbb_ref,
     dtb_ref, a_ref, ex_ref, h0_ref) = refs[:15]
    if final:
        dskip_ref, z_ref, yprev_ref, norm_ref, y_ref, hfin_ref, st_ref, ybuf_ref = refs[15:]
    else:
        y_ref, hfin_ref, st_ref = refs[15:]
        ybuf_ref = y_ref
    L = SSD_CHUNK
    heads = SSD_W // SSD_HEAD_DIM
    per_lane = LANE // SSD_HEAD_DIM
    n_pairs = heads // per_lane
    pairs_per_group = n_pairs // SSD_GROUPS
    c = pl.program_id(1)
    cc = (nc - 1 - c) if reverse else c

    @pl.when(c == 0)
    def _():
        st_ref[...] = h0_ref[...]

    xs = _conv_silu(xp_ref, xc_ref, xn_ref, cwx_ref, cbx_ref, cc > 0, cc < nc - 1)
    bcm = _conv_silu(bp_ref, bc_ref, bn_ref, cwb_ref, cbb_ref, cc > 0, cc < nc - 1)

    raw = dt_ref[...] + dtb_ref[...]
    dt = jnp.maximum(raw, 0.0) + jnp.log(1.0 + jnp.exp(-jnp.abs(raw)))
    ri = lax.broadcasted_iota(jnp.int32, (L, L), 0)
    ci = lax.broadcasted_iota(jnp.int32, (L, L), 1)
    causal = (ci >= ri) if reverse else (ci <= ri)
    tri = jnp.where(causal, 1.0, 0.0).astype(f32)
    acum = jnp.dot(tri, dt * a_ref[...], preferred_element_type=f32, precision=HIGHEST)
    acum_t = acum.T
    wide = jnp.dot(jnp.concatenate([acum, dt], axis=0), ex_ref[...], preferred_element_type=f32,
                   precision=HIGHEST)
    acum_x, dt_x = wide[:L], wide[L:]
    last = 0 if reverse else L - 1
    lane0 = heads if reverse else 0
    tot_x = acum_x[last:last + 1, :]
    xdt = xs * dt_x
    xdt_end = (xdt * jnp.exp(tot_x - acum_x)).astype(bf16)
    in_scale = jnp.exp(acum_x)
    st_scale = jnp.exp(tot_x)
    low_half = lax.broadcasted_iota(jnp.int32, (L, LANE), 1) < SSD_HEAD_DIM

    for g in range(SSD_GROUPS):
        bm_f = bcm[:, g * SSD_STATE:(g + 1) * SSD_STATE]
        bm = bm_f.astype(bf16)
        bm_t = bm_f.T.astype(bf16)
        cm_off = SSD_GROUPS * SSD_STATE
        cm = bcm[:, cm_off + g * SSD_STATE: cm_off + (g + 1) * SSD_STATE].astype(bf16)
        cb = lax.dot_general(cm, bm, (((1,), (1,)), ((), ())), preferred_element_type=f32)
        cb = jnp.where(causal, cb, 0.0)
        for pp in range(pairs_per_group):
            pr = g * pairs_per_group + pp
            cols = slice(pr * LANE, (pr + 1) * LANE)
            ws = []
            for hh in range(per_lane):
                ln = lane0 + pr * per_lane + hh
                seg = acum[:, ln:ln + 1] - acum_t[ln:ln + 1, :]
                ws.append((cb * jnp.exp(jnp.minimum(seg, 0.0))).astype(bf16))
            x2 = xdt[:, cols]
            xbd = jnp.concatenate([jnp.where(low_half, x2, 0.0).astype(bf16),
                                   jnp.where(low_half, 0.0, x2).astype(bf16)], axis=0)
            y = jnp.dot(jnp.concatenate(ws, axis=1), xbd, preferred_element_type=f32)
            st = st_ref[pr]
            y = y + jnp.dot(cm, st.astype(bf16), preferred_element_type=f32) * in_scale[:, cols]
            st_ref[pr] = st * st_scale[:, cols] + jnp.dot(bm_t, xdt_end[:, cols], preferred_element_type=f32)
            ybuf_ref[:, cols] = y

    @pl.when(c == nc - 1)
    def _():
        hfin_ref[...] = st_ref[...]

    if final:
        yt = (ybuf_ref[...] + yprev_ref[...] + dskip_ref[...] * xs) * _silu(z_ref[...])
        yn = yt * lax.rsqrt(jnp.mean(yt * yt, axis=-1, keepdims=True) + EPS) * norm_ref[...]
        y_ref[...] = yn.astype(y_ref.dtype)


def ssd_state_to_lanes(h):
    n, heads, P, N = h.shape
    per_lane = LANE // P
    return h.reshape(n, heads // per_lane, per_lane, P, N).transpose(0, 1, 4, 2, 3).reshape(
        n, heads // per_lane, N, per_lane * P)


def ssd_state_from_lanes(h):
    n, pairs, N, w = h.shape
    per_lane = w // SSD_HEAD_DIM
    return h.reshape(n, pairs, N, per_lane, SSD_HEAD_DIM).transpose(0, 1, 3, 4, 2).reshape(
        n, pairs * per_lane, SSD_HEAD_DIM, N)


def ssd_pass(proj, cols, prm, h0, n_seq, T, row0, reverse, y_prev=None):
    final = y_prev is not None
    L = SSD_CHUNK
    nc = T // L
    bc_w = 2 * SSD_GROUPS * SSD_STATE
    heads = SSD_W // SSD_HEAD_DIM
    assert LANE // SSD_HEAD_DIM == 2 and (heads // 2) % SSD_GROUPS == 0
    hb = L // SUBLANE
    c0 = row0 // L
    last_blk = (proj.shape[0] // SUBLANE) - 1

    def chunk(c):
        return (nc - 1 - c) if reverse else c

    def cur(w, col):
        return pl.BlockSpec((L, w), lambda s, c: (c0 + s * nc + chunk(c), col))

    def prev(w, col):
        return pl.BlockSpec((SUBLANE, w), lambda s, c: (jnp.maximum((c0 + s * nc + chunk(c)) * hb - 1, 0), col))

    def nxt(w, col):
        return pl.BlockSpec((SUBLANE, w),
                            lambda s, c: (jnp.minimum((c0 + s * nc + chunk(c) + 1) * hb, last_blk), col))

    out_rows = pl.BlockSpec((L, SSD_W), lambda s, c: (s * nc + chunk(c), 0))
    full = lambda a: pl.BlockSpec(a.shape, lambda s, c: (0,) * a.ndim)
    st_spec = pl.BlockSpec((None,) + h0.shape[1:], lambda s, c: (s, 0, 0, 0))
    d = 1 if reverse else 0
    consts = [prm["cw_x"], prm["cb_x"], prm["cw_bc"], prm["cb_bc"], prm["dt_bias"], prm["a_lane"], prm["expand"][d]]
    in_specs = ([prev(SSD_W, cols["xs"]), cur(SSD_W, cols["xs"]), nxt(SSD_W, cols["xs"]),
                 prev(bc_w, cols["bc"]), cur(bc_w, cols["bc"]), nxt(bc_w, cols["bc"]), cur(LANE, cols["dt"])]
                + [full(a) for a in consts] + [st_spec])
    args = [proj] * 7 + consts + [h0]
    scratch = [pltpu.VMEM(h0.shape[1:], f32)]
    if final:
        in_specs += [full(prm["dskip"]), cur(SSD_W, cols["z"]), out_rows, full(prm["norm"])]
        args += [prm["dskip"], proj, y_prev, prm["norm"]]
        scratch.append(pltpu.VMEM((L, SSD_W), f32))
    return pl.pallas_call(
        functools.partial(_ssd_kernel, reverse, final, nc),
        grid=(n_seq, nc),
        in_specs=in_specs,
        out_specs=[out_rows, st_spec],
        out_shape=[jax.ShapeDtypeStruct((n_seq * T, SSD_W), bf16 if final else f32),
                   jax.ShapeDtypeStruct(h0.shape, f32)],
        scratch_shapes=scratch,
        compiler_params=_cparams("parallel", "arbitrary"),
        name="ssd_final" if final else "ssd_first",
    )(*args)


def _hgrn_kernel(reverse, final, nblk, *refs):
    hq_ref, hf_ref, hi_ref, lb_ref, s0_ref = refs[:5]
    if final:
        hg_ref, oprev_ref, norm_ref, y_ref, sfin_ref, st_ref, q_s, k_s, bc_s, f_s, o_s = refs[5:]
    else:
        y_ref, sfin_ref, st_ref, q_s, k_s, bc_s, f_s = refs[5:]
        o_s = y_ref
    R = HGRN_ROWS
    C = HGRN_CHUNK
    nsub = R // C
    heads = HGRN_W // HGRN_DK
    DK = HGRN_DK
    b = pl.program_id(1)

    @pl.when(b == 0)
    def _():
        for h in range(heads):
            st_ref[h] = s0_ref[h].T

    r = hf_ref[...]
    lb = lb_ref[...]
    e = jnp.exp(-jnp.abs(r))
    inv = 1.0 / (1.0 + e)
    sig_pos = jnp.where(r >= 0, inv, e * inv)
    sig_neg = jnp.where(r >= 0, e * inv, inv)
    f = lb + (1.0 - lb) * sig_pos
    f_s[...] = f
    logf = jnp.log(f)
    k_s[...] = (1.0 - lb) * sig_neg
    q_s[...] = _silu(hq_ref[...])
    ri = lax.broadcasted_iota(jnp.int32, (R, R), 0)
    ci = lax.broadcasted_iota(jnp.int32, (R, R), 1)
    same = (ri // C) == (ci // C)
    tri = jnp.where(same & ((ci >= ri) if reverse else (ci <= ri)), 1.0, 0.0).astype(f32)
    bc_s[...] = jnp.dot(tri, logf, preferred_element_type=f32, precision=HIGHEST)

    ii = lax.broadcasted_iota(jnp.int32, (SUBLANE, DK), 0)
    last = 0 if reverse else C - 1

    for t in range(nsub):
        r0 = ((nsub - 1 - t) if reverse else t) * C
        rows = pl.ds(r0, C)
        for h in range(heads):
            hs = slice(h * DK, (h + 1) * DK)
            q = q_s[rows, hs]
            k = k_s[rows, hs]
            bc = bc_s[rows, hs]
            v = hi_ref[rows, hs]
            st = st_ref[h]
            o = lax.dot_general((q * jnp.exp(bc)).astype(bf16), st.astype(bf16), (((1,), (1,)), ((), ())),
                                preferred_element_type=f32)
            groups = [slice(gi * SUBLANE, (gi + 1) * SUBLANE) for gi in range(C // SUBLANE)]
            og = [o[s] for s in groups]
            qd = [None] * len(groups)
            for j in (range(C) if reverse else range(C - 1, -1, -1)):
                gj, jj = divmod(j, SUBLANE)
                step = j - 1 if reverse else j + 1
                k_row = k_s[pl.ds(r0 + j, 1), hs]
                v_row = hi_ref[pl.ds(r0 + j, 1), hs]
                f_row = f_s[pl.ds(r0 + step, 1), hs] if 0 <= step < C else None
                for gi in (range(gj + 1) if reverse else range(gj, len(groups))):
                    prev = None if qd[gi] is None else qd[gi] * f_row
                    if gi == gj:
                        qd[gi] = jnp.where(ii == jj, q[groups[gi]], 0.0 if prev is None else prev)
                    else:
                        qd[gi] = prev
                    a = jnp.sum(qd[gi] * k_row, axis=-1, keepdims=True)
                    og[gi] = og[gi] + a * v_row
            o_s[rows, hs] = jnp.concatenate(og, axis=0)
            bl = bc[last:last + 1, :]
            kd = (k * jnp.exp(bl - bc)).astype(bf16)
            upd = lax.dot_general(v.astype(bf16), kd, (((0,), (0,)), ((), ())), preferred_element_type=f32)
            st_ref[h] = st * jnp.exp(bl) + upd

    @pl.when(b == nblk - 1)
    def _():
        for h in range(heads):
            sfin_ref[h] = st_ref[h].T

    if final:
        ot = o_s[...] + oprev_ref[...]
        on = ot * lax.rsqrt(jnp.mean(ot * ot, axis=-1, keepdims=True) + EPS) * norm_ref[...]
        y_ref[...] = (on * _silu(hg_ref[...])).astype(y_ref.dtype)


def hgrn_pass(ph, f_col, lb, s0, n_seq, T, row0, reverse, o_prev=None, norm_g=None):
    final = o_prev is not None
    R = HGRN_ROWS
    nblk = T // R
    heads = HGRN_W // HGRN_DK
    b0 = row0 // R

    def blk(b):
        return (nblk - 1 - b) if reverse else b

    def col(j):
        return pl.BlockSpec((R, HGRN_W), lambda s, b: (b0 + s * nblk + blk(b), j))

    out_rows = pl.BlockSpec((R, HGRN_W), lambda s, b: (s * nblk + blk(b), 0))
    full = lambda a: pl.BlockSpec(a.shape, lambda s, b: (0,) * a.ndim)
    st_spec = pl.BlockSpec((None, heads, HGRN_DK, HGRN_DV), lambda s, b: (s, 0, 0, 0))
    in_specs = [col(0), col(f_col), col(3), full(lb), st_spec]
    args = [ph, ph, ph, lb, s0]
    scratch = [pltpu.VMEM((heads, HGRN_DV, HGRN_DK), f32)] + [pltpu.VMEM((R, HGRN_W), f32)] * 4
    if final:
        in_specs += [col(4), out_rows, full(norm_g)]
        args += [ph, o_prev, norm_g]
        scratch.append(pltpu.VMEM((R, HGRN_W), f32))
    return pl.pallas_call(
        functools.partial(_hgrn_kernel, reverse, final, nblk),
        grid=(n_seq, nblk),
        in_specs=in_specs,
        out_specs=[out_rows, st_spec],
        out_shape=[jax.ShapeDtypeStruct((n_seq * T, HGRN_W), bf16 if final else f32),
                   jax.ShapeDtypeStruct((n_seq, heads, HGRN_DK, HGRN_DV), f32)],
        scratch_shapes=scratch,
        compiler_params=_cparams("parallel", "arbitrary"),
        name="hgrn_final" if final else "hgrn_first",
    )(*args)


def _row_copy(src_hbm, dst, sem, src_row, dst_row):
    return pltpu.make_async_copy(src_hbm.at[pl.ds(src_row, 1), :], dst.at[pl.ds(dst_row, 1), :], sem)


def _moe_gather_kernel(nv_ref, idx_ref, idx_next_ref, h_hbm, o_ref, buf, sem):
    n = buf.shape[1]
    t = pl.program_id(0)
    n_valid = nv_ref[0]
    slot = t % 2

    def issue(idx, s):
        def start(r, c):
            _row_copy(h_hbm, buf.at[s], sem.at[s], idx[0, 0, r], r).start()
            return c

        lax.fori_loop(0, n, start, 0)

    @pl.when((t == 0) & (n_valid > 0))
    def _():
        issue(idx_ref, 0)

    @pl.when(t + 1 < n_valid)
    def _():
        issue(idx_next_ref, 1 - slot)

    @pl.when(t < n_valid)
    def _():
        def wait(r, c):
            _row_copy(h_hbm, buf.at[slot], sem.at[slot], 0, r).wait()
            return c

        lax.fori_loop(0, n, wait, 0)
        o_ref[...] = buf[slot].astype(o_ref.dtype)

    @pl.when(t >= n_valid)
    def _():
        o_ref[...] = jnp.zeros(o_ref.shape, o_ref.dtype)


def moe_gather(n_valid, h2, slot_tok):
    D = h2.shape[1]
    n_tiles, _, tm = slot_tok.shape
    return pl.pallas_call(
        _moe_gather_kernel,
        grid_spec=pltpu.PrefetchScalarGridSpec(
            num_scalar_prefetch=1, grid=(n_tiles,),
            in_specs=[pl.BlockSpec((1, 1, tm), lambda t, nv: (t, 0, 0), memory_space=pltpu.SMEM),
                      pl.BlockSpec((1, 1, tm), lambda t, nv: (jnp.minimum(t + 1, n_tiles - 1), 0, 0),
                                   memory_space=pltpu.SMEM),
                      pl.BlockSpec(memory_space=pl.ANY)],
            out_specs=pl.BlockSpec((tm, D), lambda t, nv: (t, 0)),
            scratch_shapes=[pltpu.VMEM((2, tm, D), h2.dtype), pltpu.SemaphoreType.DMA((2,))]),
        out_shape=jax.ShapeDtypeStruct((n_tiles * tm, D), bf16),
        compiler_params=_cparams("arbitrary"),
        name="moe_gather",
    )(n_valid, slot_tok, slot_tok, h2)


def _moe_up_kernel(te_ref, nv_ref, x_ref, wg_ref, wu_ref, o_ref):
    @pl.when(pl.program_id(0) < nv_ref[0])
    def _():
        x = x_ref[...]
        a = jnp.dot(x, wg_ref[...], preferred_element_type=f32)
        u = jnp.dot(x, wu_ref[...], preferred_element_type=f32)
        o_ref[...] = (_silu(a) * u).astype(o_ref.dtype)

    @pl.when(pl.program_id(0) >= nv_ref[0])
    def _():
        o_ref[...] = jnp.zeros(o_ref.shape, o_ref.dtype)


def moe_up(tile_e, n_valid, xs, wg, wu):
    P, D = xs.shape
    E, _, F = wg.shape
    tm = MOE_TILE
    wspec = pl.BlockSpec((None, D, F), lambda t, te, nv: (te[t], 0, 0), pipeline_mode=pl.Buffered(1))
    return pl.pallas_call(
        _moe_up_kernel,
        grid_spec=pltpu.PrefetchScalarGridSpec(
            num_scalar_prefetch=2, grid=(P // tm,),
            in_specs=[pl.BlockSpec((tm, D), lambda t, te, nv: (t, 0)), wspec, wspec],
            out_specs=pl.BlockSpec((tm, F), lambda t, te, nv: (t, 0))),
        out_shape=jax.ShapeDtypeStruct((P, F), bf16),
        compiler_params=_cparams("arbitrary"),
        name="moe_up",
    )(tile_e, n_valid, xs, wg, wu)


def _moe_down_kernel(te_ref, nv_ref, a_ref, w_ref, o_ref):
    @pl.when(pl.program_id(0) < nv_ref[0])
    def _():
        o_ref[...] = jnp.dot(a_ref[...], w_ref[...], preferred_element_type=f32)

    @pl.when(pl.program_id(0) >= nv_ref[0])
    def _():
        o_ref[...] = jnp.zeros(o_ref.shape, o_ref.dtype)


def moe_down(tile_e, n_valid, act, wd):
    P, F = act.shape
    E, _, D = wd.shape
    tm = MOE_TILE
    return pl.pallas_call(
        _moe_down_kernel,
        grid_spec=pltpu.PrefetchScalarGridSpec(
            num_scalar_prefetch=2, grid=(P // tm,),
            in_specs=[pl.BlockSpec((tm, F), lambda t, te, nv: (t, 0)),
                      pl.BlockSpec((None, F, D), lambda t, te, nv: (te[t], 0, 0), pipeline_mode=pl.Buffered(1))],
            out_specs=pl.BlockSpec((tm, D), lambda t, te, nv: (t, 0))),
        out_shape=jax.ShapeDtypeStruct((P, D), f32),
        compiler_params=_cparams("arbitrary"),
        name="moe_down",
    )(tile_e, n_valid, act, wd)


def _moe_combine_kernel(pos_ref, pos_next_ref, w_ref, x_ref, g_ref, ys_hbm, o_ref, buf, sem):
    n = x_ref.shape[0]
    i = pl.program_id(0)
    slot = i % 2

    def issue(pos, s):
        def start(r, c):
            for kk in range(TOP_K):
                _row_copy(ys_hbm, buf.at[s, kk], sem.at[s], pos[0, kk, r], r).start()
            return c

        lax.fori_loop(0, n, start, 0)

    @pl.when(i == 0)
    def _():
        issue(pos_ref, 0)

    @pl.when(i + 1 < pl.num_programs(0))
    def _():
        issue(pos_next_ref, 1 - slot)

    def wait(r, c):
        for kk in range(TOP_K):
            _row_copy(ys_hbm, buf.at[slot, kk], sem.at[slot], 0, r).wait()
        return c

    lax.fori_loop(0, n, wait, 0)
    w = w_ref[...]
    y = w[:, 0:1] * buf[slot, 0]
    for kk in range(1, TOP_K):
        y = y + w[:, kk:kk + 1] * buf[slot, kk]
    o_ref[...] = x_ref[...] + g_ref[...] * y


def moe_combine(x, gate, ys, pos, top_w):
    M, D = x.shape
    n_tiles, _, tm = pos.shape
    return pl.pallas_call(
        _moe_combine_kernel,
        grid=(n_tiles,),
        in_specs=[pl.BlockSpec((1, TOP_K, tm), lambda i: (i, 0, 0), memory_space=pltpu.SMEM),
                  pl.BlockSpec((1, TOP_K, tm), lambda i: (jnp.minimum(i + 1, n_tiles - 1), 0, 0),
                               memory_space=pltpu.SMEM),
                  pl.BlockSpec((tm, TOP_K), lambda i: (i, 0)),
                  pl.BlockSpec((tm, D), lambda i: (i, 0)),
                  pl.BlockSpec((None, 1, D), lambda i: (_group_of_tile(i, tm), 0, 0)),
                  pl.BlockSpec(memory_space=pl.ANY)],
        out_specs=pl.BlockSpec((tm, D), lambda i: (i, 0)),
        out_shape=jax.ShapeDtypeStruct((M, D), f32),
        scratch_shapes=[pltpu.VMEM((2, TOP_K, tm, D), f32), pltpu.SemaphoreType.DMA((2,))],
        compiler_params=_cparams("arbitrary"),
        name="moe_combine",
    )(pos, pos, top_w, x, gate, ys)


def moe_ffn(x, norm_g, sc, sh, gate, w_router, wg, wu, wd):
    M, D = x.shape
    E = w_router.shape[1]
    wr = jnp.zeros((D, LANE), f32).at[:, :E].set(w_router)
    h2, logits = norm_router(x, norm_g, sc, sh, wr)
    top_v, top_i = lax.top_k(logits[:, :E], TOP_K)
    top_w = jax.nn.softmax(top_v, axis=-1)
    nk = M * TOP_K
    n_tiles = -(-nk // MOE_TILE) + E
    n_slots = n_tiles * MOE_TILE
    e_flat = top_i.reshape(-1).astype(jnp.int32)
    onehot = (e_flat[:, None] == jnp.arange(E, dtype=jnp.int32)[None, :]).astype(jnp.int32)
    running = jnp.cumsum(onehot, axis=0)
    rank = jnp.sum((running - onehot) * onehot, axis=1)
    counts = running[-1]
    padded = (counts + MOE_TILE - 1) // MOE_TILE * MOE_TILE
    start = jnp.cumsum(counts) - counts
    pend = jnp.cumsum(padded)
    pstart = pend - padded
    pos = jnp.sum(onehot * pstart[None, :], axis=1) + rank
    n_valid = (pend[-1:] // MOE_TILE).astype(jnp.int32)
    tile_e = jnp.minimum(jnp.searchsorted(pend, jnp.arange(n_tiles, dtype=jnp.int32) * MOE_TILE, side='right'),
                         E - 1).astype(jnp.int32)
    order = jnp.argsort(e_flat, stable=True).astype(jnp.int32)
    slot = jnp.arange(n_slots, dtype=jnp.int32)
    slot_e = jnp.repeat(tile_e, MOE_TILE)
    r = slot - pstart[slot_e]
    live = r < counts[slot_e]
    slot_tok = jnp.where(live, order[jnp.where(live, start[slot_e] + r, 0)] // TOP_K, slot % M)
    xs = moe_gather(n_valid, h2, slot_tok.reshape(n_tiles, 1, MOE_TILE))
    act = moe_up(tile_e, n_valid, xs, wg, wu)
    ys = moe_down(tile_e, n_valid, act, wd)
    tc = _row_tile((256, 128))
    pos_t = pos.reshape(M // tc, tc, TOP_K).transpose(0, 2, 1)
    return moe_combine(x, gate, ys, pos_t, top_w)


def _rope_tables():
    rows = DEC_SEQ // GRID_W
    row = jnp.repeat(jnp.arange(rows), GRID_W)
    col = jnp.tile(jnp.arange(GRID_W), rows)
    quarter = HEAD_DIM // 4
    inv = ROPE_THETA ** (-jnp.arange(quarter, dtype=f32) / quarter)
    ar = row.astype(f32)[:, None] * inv[None, :]
    ac = col.astype(f32)[:, None] * inv[None, :]
    cos = jnp.concatenate([jnp.cos(ar), jnp.cos(ar), jnp.cos(ac), jnp.cos(ac)], axis=-1)
    sin = jnp.concatenate([-jnp.sin(ar), jnp.sin(ar), -jnp.sin(ac), jnp.sin(ac)], axis=-1)
    rep = LANE // HEAD_DIM
    return jnp.tile(cos, (1, rep)), jnp.tile(sin, (1, rep))


def kernel(x_prompt, x_sample, cache_k, cache_v, state_ssd, state_hgrn, c, c_ctx, w_ada, b_ada, norm1_g, norm2_g, w_in, attn_sink, ssd_conv_w, ssd_conv_b, ssd_dt_bias, ssd_a_log, ssd_d, ssd_norm_g, hgrn_lb, hgrn_norm_g, w_gate, w_br_att, w_br_ssd, w_br_hgrn, w_out, ffn_wg, ffn_wu, ffn_wd, router_w, moe_wg, moe_wu, moe_wd, final_g):
    att_w, kv_w, ssd_heads, conv_ch, hgrn_heads, nc, nl = _dims()
    D = D_MODEL
    G = 1 + DEC_BATCH
    g_pad = -(-G // SUBLANE) * SUBLANE
    x = jnp.concatenate([x_prompt.reshape(nc, D), x_sample.reshape(nl, D)], axis=0)
    m = jnp.zeros((g_pad, D), f32).at[0].set(c_ctx).at[1:G].set(c)
    mods = ada_mod(m, w_ada, b_ada)
    sm = jax.nn.softmax(hgrn_lb.astype(f32), axis=1)
    lb_all = jnp.cumsum(sm, axis=1) - sm[:, :1]
    cos, sin = _rope_tables()

    cuts = np.cumsum([0, att_w, kv_w, kv_w, SSD_W, conv_ch, 2 * ssd_heads, 5 * HGRN_W, GATE_RANK])
    bc_w = conv_ch - SSD_W
    off, o = {}, 0
    for name, w in (("hgrn", 5 * HGRN_W), ("q", att_w), ("z", SSD_W), ("xs", SSD_W), ("bc", bc_w),
                    ("kv", 2 * kv_w), ("gc", GATE_RANK), ("dt", LANE)):
        off[name] = o
        o += w

    def blk(name, width):
        assert off[name] % width == 0
        return off[name] // width

    ssd_cols = {"xs": blk("xs", SSD_W), "bc": blk("bc", bc_w), "z": blk("z", SSD_W), "dt": blk("dt", LANE)}
    expand = np.zeros((2, LANE, SSD_W), np.float32)
    for d in range(2):
        for hh in range(ssd_heads):
            expand[d, d * ssd_heads + hh, hh * SSD_HEAD_DIM:(hh + 1) * SSD_HEAD_DIM] = 1.0
    expand = jnp.asarray(expand)
    lanes = jnp.zeros((1, LANE), f32)
    zero_ssd = jnp.zeros((BATCH, ssd_heads * SSD_HEAD_DIM // LANE, SSD_STATE, LANE), f32)
    zero_h = jnp.zeros((BATCH, hgrn_heads, HGRN_DK, HGRN_DV), f32)

    ks, vs, hs, ss = [], [], [], []
    for l in range(DEPTH):
        mod = mods[l].reshape(g_pad, 6, 1, D)
        sh1, sc1, g1, sh2, sc2, g2 = [mod[:, j] for j in range(6)]
        wl = w_in[l]
        w_proj = jnp.concatenate(
            [wl[:, cuts[6]:cuts[7]], wl[:, cuts[0]:cuts[1]], wl[:, cuts[3]:cuts[4]],
             wl[:, cuts[4]:cuts[4] + SSD_W], wl[:, cuts[4] + SSD_W:cuts[5]], wl[:, cuts[1]:cuts[3]],
             wl[:, cuts[7]:cuts[8]], wl[:, cuts[5]:cuts[6]], jnp.zeros((D, DT_PAD - 2 * ssd_heads), f32)],
            axis=1).astype(bf16)
        proj = norm_matmul(x, norm1_g[l], sc1, sh1, w_proj)

        y_att_c = attention_context(proj, blk("q", att_w), blk("kv", kv_w), attn_sink[l])
        q_rot, kv_rot = rope_latent(proj, blk("q", att_w), blk("kv", 2 * kv_w), cos, sin, nc)
        y_att_l = attention_latent(q_rot, kv_rot,
                                   cache_k[:, l].reshape(DEC_BATCH, PAST_LEN, kv_w).astype(bf16),
                                   cache_v[:, l].reshape(DEC_BATCH, PAST_LEN, kv_w).astype(bf16), attn_sink[l])
        y_att = jnp.concatenate([y_att_c, y_att_l], axis=0)
        ks.append(proj[:nc, off["kv"]:off["kv"] + kv_w].reshape(BATCH, SEQ, N_KV, HEAD_DIM))
        vs.append(proj[:nc, off["kv"] + kv_w:off["kv"] + 2 * kv_w].reshape(BATCH, SEQ, N_KV, HEAD_DIM))

        prm = {
            "cw_x": ssd_conv_w[l][:, :SSD_W], "cb_x": ssd_conv_b[l][:SSD_W].reshape(1, SSD_W),
            "cw_bc": ssd_conv_w[l][:, SSD_W:], "cb_bc": ssd_conv_b[l][SSD_W:].reshape(1, bc_w),
            "dt_bias": lanes.at[0, :2 * ssd_heads].set(ssd_dt_bias[l].reshape(-1)),
            "a_lane": lanes.at[0, :2 * ssd_heads].set(-jnp.exp(ssd_a_log[l].astype(f32)).reshape(-1)),
            "expand": expand,
            "dskip": jnp.repeat(ssd_d[l].astype(f32), SSD_HEAD_DIM).reshape(1, SSD_W),
            "norm": ssd_norm_g[l].reshape(1, SSD_W),
        }
        y_ssd, h_ssd = [], []
        for (n_seq, T, row0, h0f, h0b) in (
                (BATCH, SEQ, 0, zero_ssd, zero_ssd),
                (DEC_BATCH, DEC_SEQ, nc, ssd_state_to_lanes(state_ssd[:, l, 0]), ssd_state_to_lanes(state_ssd[:, l, 1]))):
            yf, hf = ssd_pass(proj, ssd_cols, prm, h0f, n_seq, T, row0, False)
            yb, hb = ssd_pass(proj, ssd_cols, prm, h0b, n_seq, T, row0, True, y_prev=yf)
            y_ssd.append(yb)
            h_ssd.append((hf, hb))
        hs.append(jnp.stack([ssd_state_from_lanes(t) for t in h_ssd[0]], axis=1))
        y_ssd = jnp.concatenate(y_ssd, axis=0)

        assert off["hgrn"] == 0
        lb = lb_all[:, l]
        hn = hgrn_norm_g[l].reshape(1, HGRN_W)
        y_hgrn, s_hgrn = [], []
        for (n_seq, T, row0, s0f, s0b) in ((BATCH, SEQ, 0, zero_h, zero_h),
                                           (DEC_BATCH, DEC_SEQ, nc, state_hgrn[:, l, 0], state_hgrn[:, l, 1])):
            of, sf = hgrn_pass(proj, 1, lb[0:1], s0f, n_seq, T, row0, False)
            ob, sb = hgrn_pass(proj, 2, lb[1:2], s0b, n_seq, T, row0, True, o_prev=of, norm_g=hn)
            y_hgrn.append(ob)
            s_hgrn.append((sf, sb))
        ss.append(jnp.stack(s_hgrn[0], axis=1))
        y_hgrn = jnp.concatenate(y_hgrn, axis=0)

        merged = merge_branches(proj, blk("gc", GATE_RANK), y_att, y_ssd, y_hgrn, w_gate[l].astype(bf16),
                                w_br_att[l].astype(bf16), w_br_ssd[l].astype(bf16), w_br_hgrn[l].astype(bf16))
        x = matmul_residual(merged, w_out[l].astype(bf16), x, g1, "out_proj")

        i = l // 2
        if l % 2 == 0:
            act = norm_glu_up(x, norm2_g[l], sc2, sh2, ffn_wg[i].astype(bf16), ffn_wu[i].astype(bf16))
            x = matmul_residual(act, ffn_wd[i].astype(bf16), x, g2, "ffn_down")
        else:
            x = moe_ffn(x, norm2_g[l], sc2, sh2, g2, router_w[i], moe_wg[i].astype(bf16), moe_wu[i].astype(bf16),
                        moe_wd[i].astype(bf16))

    y = final_norm(x, final_g)
    y_prompt = y[:nc].reshape(BATCH, SEQ, D)
    y_sample = y[nc:].reshape(DEC_BATCH, DEC_SEQ, D)
    return (y_prompt, y_sample, jnp.stack(ks, axis=1), jnp.stack(vs, axis=1),
            jnp.stack(hs, axis=1), jnp.stack(ss, axis=1))
```

```python
import functools

import numpy as np
import jax
import jax.numpy as jnp
from jax import lax
from jax.experimental import pallas as pl
from jax.experimental.pallas import tpu as pltpu

f32 = jnp.float32
bf16 = jnp.bfloat16

D_MODEL = 4096
BATCH = 16
SEQ = 256
DEPTH = 4
DEC_BATCH = 4
DEC_SEQ = 4096
PAST_LEN = 256
GRID_W = 64
N_HEADS = 16
N_KV = 4
HEAD_DIM = 64
WINDOW = 128
BLOCK = 128
ROPE_THETA = 10000.0
SSD_W = 1024
SSD_HEAD_DIM = 64
SSD_GROUPS = 2
SSD_STATE = 128
SSD_CONV = 5
SSD_CHUNK = 128
HGRN_W = 1024
HGRN_DK = 128
HGRN_DV = 128
HGRN_CHUNK = 16
GATE_RANK = 512
D_FF = 5632
N_EXPERTS = 8
TOP_K = 2
EXPERT_FF = 1408
EPS = 1e-6

LANE = 128
SUBLANE = 8
VMEM_LIMIT = 56 * 1024 * 1024
MOE_TILE = 512
HGRN_ROWS = 128
DT_PAD = 2 * LANE
HIGHEST = lax.Precision.HIGHEST


def _cparams(*sem):
    return pltpu.CompilerParams(dimension_semantics=sem, vmem_limit_bytes=VMEM_LIMIT)


def _pick(n, prefs):
    for p in prefs:
        if n % p == 0:
            return p
    return n


def _silu(x):
    return x * (1.0 / (1.0 + jnp.exp(-x)))


def _sigmoid(x):
    return 1.0 / (1.0 + jnp.exp(-x))


def _dims():
    att_w = N_HEADS * HEAD_DIM
    kv_w = N_KV * HEAD_DIM
    ssd_heads = SSD_W // SSD_HEAD_DIM
    conv_ch = SSD_W + 2 * SSD_GROUPS * SSD_STATE
    hgrn_heads = HGRN_W // HGRN_DK
    nc = BATCH * SEQ
    nl = DEC_BATCH * DEC_SEQ
    return att_w, kv_w, ssd_heads, conv_ch, hgrn_heads, nc, nl


def _group_of_tile(i, tm):
    nc = BATCH * SEQ
    r = i * tm
    return jnp.where(r < nc, 0, 1 + (r - nc) // DEC_SEQ)


def _row_tile(prefs=(512, 256, 128)):
    nc = BATCH * SEQ
    return _pick(int(np.gcd(nc, DEC_SEQ)), prefs)


def _ada_kernel(m_ref, w_ref, b_ref, o_ref):
    a = _silu(m_ref[...]).astype(bf16)
    o_ref[...] = jnp.dot(a, w_ref[...].astype(bf16), preferred_element_type=f32) + b_ref[...]


def ada_mod(m_pad, w_ada, b_ada):
    L, D, N = w_ada.shape
    G = m_pad.shape[0]
    tn = _pick(N, (512, 256, 128))
    return pl.pallas_call(
        _ada_kernel,
        grid=(L, N // tn),
        in_specs=[pl.BlockSpec((G, D), lambda l, j: (0, 0)),
                  pl.BlockSpec((None, D, tn), lambda l, j: (l, 0, j)),
                  pl.BlockSpec((None, 1, tn), lambda l, j: (l, 0, j))],
        out_specs=pl.BlockSpec((None, G, tn), lambda l, j: (l, 0, j)),
        out_shape=jax.ShapeDtypeStruct((L, G, N), f32),
        compiler_params=_cparams("parallel", "parallel"),
        name="ada_mod",
    )(m_pad, w_ada, b_ada.reshape(L, 1, N))


def _rms_mod(x, g, sc=None, sh=None):
    y = x * lax.rsqrt(jnp.mean(x * x, axis=-1, keepdims=True) + EPS) * g
    if sc is not None:
        y = y * (1.0 + sc) + sh
    return y


def _norm_kernel(x_ref, g_ref, o_ref):
    o_ref[...] = _rms_mod(x_ref[...], g_ref[...]).astype(o_ref.dtype)


def final_norm(x, g, row0, n_rows):
    D = x.shape[1]
    tm = _row_tile((256, 128))
    off = row0 // tm
    return pl.pallas_call(
        _norm_kernel,
        grid=(n_rows // tm,),
        in_specs=[pl.BlockSpec((tm, D), lambda i: (i + off, 0)), pl.BlockSpec((1, D), lambda i: (0, 0))],
        out_specs=pl.BlockSpec((tm, D), lambda i: (i, 0)),
        out_shape=jax.ShapeDtypeStruct((n_rows, D), f32),
        compiler_params=_cparams("parallel"),
        name="final_norm",
    )(x, g.reshape(1, D))


def _cast_kernel(x_ref, o_ref):
    o_ref[...] = x_ref[...].astype(o_ref.dtype)


def cast_bf16(w):
    shape = w.shape
    w2 = w.reshape(-1, shape[-1])
    R, C = w2.shape
    tr = _pick(R, (1024, 512, 256, 128, 64, 32, 16))
    tc = _pick(C, (2048, 1408, 1024, 512, 256, 128))
    out = pl.pallas_call(
        _cast_kernel,
        grid=(R // tr, C // tc),
        in_specs=[pl.BlockSpec((tr, tc), lambda i, j: (i, j))],
        out_specs=pl.BlockSpec((tr, tc), lambda i, j: (i, j)),
        out_shape=jax.ShapeDtypeStruct((R, C), bf16),
        compiler_params=_cparams("parallel", "parallel"),
        name="cast_bf16",
    )(w2)
    return out.reshape(shape)


def _norm_router_kernel(x_ref, g_ref, sc_ref, sh_ref, wr_ref, h_ref, lg_ref):
    h = _rms_mod(x_ref[...], g_ref[...], sc_ref[...], sh_ref[...])
    h_ref[...] = h
    lg_ref[...] = jnp.dot(h, wr_ref[...], preferred_element_type=f32, precision=HIGHEST)


def norm_router(x, g, sc, sh, w_router_pad):
    M, D = x.shape
    tm = _row_tile((256, 128))
    row = pl.BlockSpec((tm, D), lambda i: (i, 0))
    grp = pl.BlockSpec((None, 1, D), lambda i: (_group_of_tile(i, tm), 0, 0))
    return pl.pallas_call(
        _norm_router_kernel,
        grid=(M // tm,),
        in_specs=[row, pl.BlockSpec((1, D), lambda i: (0, 0)), grp, grp,
                  pl.BlockSpec((D, LANE), lambda i: (0, 0))],
        out_specs=[row, pl.BlockSpec((tm, LANE), lambda i: (i, 0))],
        out_shape=[jax.ShapeDtypeStruct((M, D), f32), jax.ShapeDtypeStruct((M, LANE), f32)],
        compiler_params=_cparams("parallel"),
        name="norm_router",
    )(x, g.reshape(1, D), sc, sh, w_router_pad)


def _norm_specs(tm, D):
    grp = pl.BlockSpec((None, 1, D), lambda i, j: (_group_of_tile(i, tm), 0, 0))
    return [pl.BlockSpec((tm, D), lambda i, j: (i, 0), pipeline_mode=pl.Buffered(1)),
            pl.BlockSpec((1, D), lambda i, j: (0, 0)), grp, grp]


def _norm_rows_to(h_s, x_ref, g_ref, sc_ref, sh_ref):
    rows = _pick(x_ref.shape[0], (256, 128))

    def chunk(c, carry):
        r = pl.ds(pl.multiple_of(c * rows, rows), rows)
        h_s[r, :] = _rms_mod(x_ref[r, :], g_ref[...], sc_ref[...], sh_ref[...]).astype(bf16)
        return carry

    lax.fori_loop(0, x_ref.shape[0] // rows, chunk, 0)


def _norm_mm_kernel(x_ref, g_ref, sc_ref, sh_ref, w_ref, o_ref, h_s):
    @pl.when(pl.program_id(1) == 0)
    def _():
        _norm_rows_to(h_s, x_ref, g_ref, sc_ref, sh_ref)

    o_ref[...] = jnp.dot(h_s[...], w_ref[...], preferred_element_type=f32)


def norm_matmul(x, g, sc, sh, w):
    M, D = x.shape
    N = w.shape[1]
    tm = _row_tile((1024, 512, 256, 128))
    tn = _pick(N, (768, 1024, 512, 256, 128))
    return pl.pallas_call(
        _norm_mm_kernel,
        grid=(M // tm, N // tn),
        in_specs=_norm_specs(tm, D) + [pl.BlockSpec((D, tn), lambda i, j: (0, j))],
        out_specs=pl.BlockSpec((tm, tn), lambda i, j: (i, j)),
        out_shape=jax.ShapeDtypeStruct((M, N), f32),
        scratch_shapes=[pltpu.VMEM((tm, D), bf16)],
        compiler_params=_cparams("parallel", "arbitrary"),
        name="in_proj",
    )(x, g.reshape(1, D), sc, sh, w)


def _norm_glu_kernel(x_ref, g_ref, sc_ref, sh_ref, wg_ref, wu_ref, o_ref, h_s):
    @pl.when(pl.program_id(1) == 0)
    def _():
        _norm_rows_to(h_s, x_ref, g_ref, sc_ref, sh_ref)

    h = h_s[...]
    a = jnp.dot(h, wg_ref[...], preferred_element_type=f32)
    u = jnp.dot(h, wu_ref[...], preferred_element_type=f32)
    o_ref[...] = (_silu(a) * u).astype(o_ref.dtype)


def norm_glu_up(x, g, sc, sh, wg, wu, layer):
    M, D = x.shape
    N = wg.shape[2]
    tm = _row_tile((1024, 512, 256, 128))
    tn = _pick(N, (512, 256, 128))
    wspec = pl.BlockSpec((None, D, tn), lambda i, j: (layer, 0, j))
    return pl.pallas_call(
        _norm_glu_kernel,
        grid=(M // tm, N // tn),
        in_specs=_norm_specs(tm, D) + [wspec, wspec],
        out_specs=pl.BlockSpec((tm, tn), lambda i, j: (i, j)),
        out_shape=jax.ShapeDtypeStruct((M, N), bf16),
        scratch_shapes=[pltpu.VMEM((tm, D), bf16)],
        compiler_params=_cparams("parallel", "arbitrary"),
        name="glu_up",
    )(x, g.reshape(1, D), sc, sh, wg, wu)


def _mm_res_kernel(a_ref, w_ref, x_ref, g_ref, o_ref):
    y = jnp.dot(a_ref[...], w_ref[...], preferred_element_type=f32)
    o_ref[...] = x_ref[...] + g_ref[...] * y


def matmul_residual(a, w, layer, x, gate, name="matmul_residual"):
    M, K = a.shape
    N = w.shape[2]
    tm = _row_tile((512, 256, 128))
    tn = _pick(N, (1024, 512, 256, 128))
    return pl.pallas_call(
        _mm_res_kernel,
        grid=(N // tn, M // tm),
        in_specs=[pl.BlockSpec((tm, K), lambda j, i: (i, 0)),
                  pl.BlockSpec((None, K, tn), lambda j, i: (layer, 0, j)),
                  pl.BlockSpec((tm, tn), lambda j, i: (i, j)),
                  pl.BlockSpec((None, 1, tn), lambda j, i: (_group_of_tile(i, tm), 0, j))],
        out_specs=pl.BlockSpec((tm, tn), lambda j, i: (i, j)),
        out_shape=jax.ShapeDtypeStruct((M, N), f32),
        compiler_params=_cparams("parallel", "parallel"),
        name=name,
    )(a, w, x, gate)


def _merge_kernel(n_ctx_tiles, gc_ref, yac_ref, yal_ref, ysc_ref, ysl_ref, yhc_ref, yhl_ref,
                  wga_ref, wgs_ref, wgh_ref, wa_ref, ws_ref, wh_ref, o_ref):
    gc = gc_ref[...].astype(bf16)
    is_ctx = pl.program_id(1) < n_ctx_tiles

    def branch(yc_ref, yl_ref, wg_ref, w_ref):
        y = jnp.where(is_ctx, yc_ref[...], yl_ref[...])
        gate = _sigmoid(jnp.dot(gc, wg_ref[...], preferred_element_type=f32))
        return gate * jnp.dot(y, w_ref[...], preferred_element_type=f32)

    o_ref[...] = (branch(yac_ref, yal_ref, wga_ref, wa_ref) + branch(ysc_ref, ysl_ref, wgs_ref, ws_ref)
                  + branch(yhc_ref, yhl_ref, wgh_ref, wh_ref)).astype(o_ref.dtype)


def merge_branches(proj, gc_col, y_att, y_ssd, y_hgrn, w_gate, w_att, w_ssd, w_hgrn, layer):
    M = proj.shape[0]
    R = w_gate.shape[1]
    D = w_att.shape[2]
    tm = _row_tile((512, 256, 128))
    tn = _pick(D, (1024, 512, 256, 128))
    nj = D // tn
    n_ctx = (BATCH * SEQ) // tm
    n_lat = (DEC_BATCH * DEC_SEQ) // tm

    def pair(y):
        w = y[0].shape[1]
        return [pl.BlockSpec((tm, w), lambda j, i: (jnp.minimum(i, n_ctx - 1), 0)),
                pl.BlockSpec((tm, w), lambda j, i: (jnp.clip(i - n_ctx, 0, n_lat - 1), 0))]

    def gate_cols(b):
        return pl.BlockSpec((None, R, tn), lambda j, i: (layer, 0, j + b * nj))

    def cols(w):
        return pl.BlockSpec((None, w.shape[1], tn), lambda j, i: (layer, 0, j))

    return pl.pallas_call(
        functools.partial(_merge_kernel, n_ctx),
        grid=(nj, M // tm),
        in_specs=[pl.BlockSpec((tm, R), lambda j, i: (i, gc_col))] + pair(y_att) + pair(y_ssd) + pair(y_hgrn)
                 + [gate_cols(0), gate_cols(1), gate_cols(2), cols(w_att), cols(w_ssd), cols(w_hgrn)],
        out_specs=pl.BlockSpec((tm, tn), lambda j, i: (i, j)),
        out_shape=jax.ShapeDtypeStruct((M, D), bf16),
        compiler_params=_cparams("parallel", "parallel"),
        name="merge_branches",
    )(proj, *y_att, *y_ssd, *y_hgrn, w_gate, w_gate, w_gate, w_att, w_ssd, w_hgrn)


def _softmax_parts(parts, sink):
    m = sink
    for s in parts:
        m = jnp.maximum(m, jnp.max(s, axis=-1, keepdims=True))
    ps = [jnp.exp(s - m) for s in parts]
    den = jnp.exp(sink - m)
    for p in ps:
        den = den + jnp.sum(p, axis=-1, keepdims=True)
    return ps, 1.0 / den


def _attend_group(sink_ref, kv, q_heads, pieces, o_ref):
    g = len(q_heads)
    outs = []
    for gi, q in enumerate(q_heads):
        scores = []
        for k, _, bias in pieces:
            s = lax.dot_general(q, k, (((1,), (1,)), ((), ())), preferred_element_type=f32)
            scores.append(s if bias is None else s + bias)
        ps, inv = _softmax_parts(scores, sink_ref[kv * g + gi])
        o = jnp.dot(ps[0].astype(bf16), pieces[0][1], preferred_element_type=f32)
        for p, (_, v, _) in zip(ps[1:], pieces[1:]):
            o = o + jnp.dot(p.astype(bf16), v, preferred_element_type=f32)
        outs.append(o * inv)
    per_lane = LANE // HEAD_DIM
    for pi in range(g // per_lane):
        tile = jnp.concatenate(outs[pi * per_lane:(pi + 1) * per_lane], axis=1)
        c0 = (kv * g + pi * per_lane) * HEAD_DIM
        o_ref[:, c0:c0 + LANE] = tile.astype(o_ref.dtype)


def _attn_ctx_kernel(sink_ref, q_ref, k_ref, v_ref, o_ref):
    g = N_HEADS // N_KV
    scale = HEAD_DIM ** -0.5
    for kv in range(N_KV):
        sl = slice(kv * HEAD_DIM, (kv + 1) * HEAD_DIM)
        k = k_ref[:, sl].astype(bf16)
        v = v_ref[:, sl].astype(bf16)
        qs = [(q_ref[:, (kv * g + gi) * HEAD_DIM:(kv * g + gi + 1) * HEAD_DIM] * scale).astype(bf16)
              for gi in range(g)]
        _attend_group(sink_ref, kv, qs, [(k, v, None)], o_ref)


def attention_context(proj, q_col, k_col, sink):
    att_w, kv_w = N_HEADS * HEAD_DIM, N_KV * HEAD_DIM
    return pl.pallas_call(
        _attn_ctx_kernel,
        grid=(BATCH,),
        in_specs=[pl.BlockSpec(memory_space=pltpu.SMEM),
                  pl.BlockSpec((SEQ, att_w), lambda b: (b, q_col)),
                  pl.BlockSpec((SEQ, kv_w), lambda b: (b, k_col)),
                  pl.BlockSpec((SEQ, kv_w), lambda b: (b, k_col + 1))],
        out_specs=pl.BlockSpec((SEQ, att_w), lambda b: (b, 0)),
        out_shape=jax.ShapeDtypeStruct((BATCH * SEQ, att_w), bf16),
        compiler_params=_cparams("parallel"),
        name="attention_context",
    )(sink, proj, proj, proj)


def _rope_kernel(n_q, n_k, q_ref, kv_ref, cos_ref, sin_ref, qo_ref, ko_ref, vt_ref):
    cos = cos_ref[...]
    sin = sin_ref[...]
    quarter = HEAD_DIM // 4
    lane = lax.broadcasted_iota(jnp.int32, cos.shape, 1)
    first = (lane % (2 * quarter)) < quarter

    def rot(x):
        swapped = jnp.where(first, pltpu.roll(x, LANE - quarter, axis=1), pltpu.roll(x, quarter, axis=1))
        return x * cos + swapped * sin

    scale = HEAD_DIM ** -0.5
    for j in range(n_q):
        sl = slice(j * LANE, (j + 1) * LANE)
        qo_ref[:, sl] = (rot(q_ref[:, sl]) * scale).astype(qo_ref.dtype)
    for j in range(n_k):
        sl = slice(j * LANE, (j + 1) * LANE)
        ko_ref[:, sl] = rot(kv_ref[:, sl]).astype(ko_ref.dtype)
    vt_ref[...] = kv_ref[:, n_k * LANE:].T.astype(vt_ref.dtype)


def rope_latent(proj, q_col, kv_col, cos, sin, row0):
    att_w, kv_w = N_HEADS * HEAD_DIM, N_KV * HEAD_DIM
    nl = DEC_BATCH * DEC_SEQ
    tm = _pick(DEC_SEQ, (512, 256, 128))
    off = row0 // tm
    per_seq = DEC_SEQ // tm
    tab = pl.BlockSpec((tm, LANE), lambda i: (i % per_seq, 0))
    return pl.pallas_call(
        functools.partial(_rope_kernel, att_w // LANE, kv_w // LANE),
        grid=(nl // tm,),
        in_specs=[pl.BlockSpec((tm, att_w), lambda i: (i + off, q_col)),
                  pl.BlockSpec((tm, 2 * kv_w), lambda i: (i + off, kv_col)), tab, tab],
        out_specs=[pl.BlockSpec((tm, att_w), lambda i: (i, 0)), pl.BlockSpec((tm, kv_w), lambda i: (i, 0)),
                   pl.BlockSpec((kv_w, tm), lambda i: (0, i))],
        out_shape=[jax.ShapeDtypeStruct((nl, att_w), bf16), jax.ShapeDtypeStruct((nl, kv_w), bf16),
                   jax.ShapeDtypeStruct((kv_w, nl), bf16)],
        compiler_params=_cparams("parallel"),
        name="rope_latent",
    )(proj, proj, cos, sin)


def _attn_lat_kernel(nb, sink_ref, q_ref, kp_ref, kc_ref, kn_ref, vtp_ref, vtc_ref, vtn_ref, kctx_ref, vtctx_ref,
                     o_ref):
    n = pl.program_id(1)
    g = N_HEADS // N_KV
    kj = lax.broadcasted_iota(jnp.int32, (3 * BLOCK, BLOCK), 0)
    qi = lax.broadcasted_iota(jnp.int32, (3 * BLOCK, BLOCK), 1)
    valid = (jnp.abs(kj - BLOCK - qi) <= WINDOW)
    valid = valid & ((kj >= BLOCK) | (n > 0)) & ((kj < 2 * BLOCK) | (n < nb - 1))
    bias = jnp.where(valid, 0.0, -1e30).astype(f32)
    k_loc_all = jnp.concatenate([kp_ref[...], kc_ref[...], kn_ref[...]], axis=0)
    vt_loc_all = jnp.concatenate([vtp_ref[...], vtc_ref[...], vtn_ref[...]], axis=1)
    nt = (((1,), (1,)), ((), ()))
    per_lane = LANE // HEAD_DIM
    for kv in range(N_KV):
        sl = slice(kv * HEAD_DIM, (kv + 1) * HEAD_DIM)
        k_loc, k_ctx = k_loc_all[:, sl], kctx_ref[:, sl]
        vt_loc, vt_ctx = vt_loc_all[sl, :], vtctx_ref[sl, :]
        outs = []
        for gi in range(g):
            h = kv * g + gi
            q = q_ref[:, h * HEAD_DIM:(h + 1) * HEAD_DIM]
            s_ctx = lax.dot_general(k_ctx, q, nt, preferred_element_type=f32)
            s_loc = lax.dot_general(k_loc, q, nt, preferred_element_type=f32) + bias
            sink = sink_ref[h]
            m = jnp.maximum(jnp.maximum(jnp.max(s_ctx, axis=0, keepdims=True),
                                        jnp.max(s_loc, axis=0, keepdims=True)), sink)
            p_ctx = jnp.exp(s_ctx - m)
            p_loc = jnp.exp(s_loc - m)
            den = (jnp.sum(p_ctx, axis=0, keepdims=True) + jnp.sum(p_loc, axis=0, keepdims=True)
                   + jnp.exp(sink - m))
            o_t = (jnp.dot(vt_ctx, p_ctx.astype(bf16), preferred_element_type=f32)
                   + jnp.dot(vt_loc, p_loc.astype(bf16), preferred_element_type=f32))
            outs.append(o_t * (1.0 / den))
        for pi in range(g // per_lane):
            tile = jnp.concatenate(outs[pi * per_lane:(pi + 1) * per_lane], axis=0).T
            c0 = (kv * g + pi * per_lane) * HEAD_DIM
            o_ref[:, c0:c0 + LANE] = tile.astype(o_ref.dtype)


def attention_latent(q_rot, k_rot, vt, k_ctx, vt_ctx, sink):
    att_w, kv_w = q_rot.shape[1], k_rot.shape[1]
    nb = DEC_SEQ // BLOCK
    past = k_ctx.shape[1]
    prev = lambda n: jnp.maximum(n - 1, 0)
    same = lambda n: n
    nxt = lambda n: jnp.minimum(n + 1, nb - 1)
    kspec = lambda f: pl.BlockSpec((BLOCK, kv_w), lambda b, n: (b * nb + f(n), 0))
    vspec = lambda f: pl.BlockSpec((kv_w, BLOCK), lambda b, n: (0, b * nb + f(n)))
    return pl.pallas_call(
        functools.partial(_attn_lat_kernel, nb),
        grid=(DEC_BATCH, nb),
        in_specs=[pl.BlockSpec(memory_space=pltpu.SMEM),
                  pl.BlockSpec((BLOCK, att_w), lambda b, n: (b * nb + n, 0)),
                  kspec(prev), kspec(same), kspec(nxt), vspec(prev), vspec(same), vspec(nxt),
                  pl.BlockSpec((None, past, kv_w), lambda b, n: (b, 0, 0)),
                  pl.BlockSpec((None, kv_w, past), lambda b, n: (b, 0, 0))],
        out_specs=pl.BlockSpec((BLOCK, att_w), lambda b, n: (b * nb + n, 0)),
        out_shape=jax.ShapeDtypeStruct((DEC_BATCH * DEC_SEQ, att_w), bf16),
        compiler_params=_cparams("parallel", "parallel"),
        name="attention_latent",
    )(sink, q_rot, k_rot, k_rot, k_rot, vt, vt, vt, k_ctx, vt_ctx)


def _conv_silu(p_ref, c_ref, n_ref, w_ref, b_ref, has_prev, has_next):
    L = SSD_CHUNK
    lo = SUBLANE - SSD_CONV // 2
    xp = jnp.where(has_prev, p_ref[...], 0.0)
    xn = jnp.where(has_next, n_ref[...], 0.0)
    ext = jnp.concatenate([xp, c_ref[...], xn], axis=0)
    acc = b_ref[...] + ext[lo: lo + L] * w_ref[0:1, :]
    for k in range(1, SSD_CONV):
        acc = acc + ext[lo + k: lo + k + L] * w_ref[k:k + 1, :]
    return _silu(acc)


def _ssd_kernel(reverse, final, nc, *refs):
    (xp_ref, xc_ref, xn_ref, bp_ref, bc_ref, bn_ref, dt_ref, cwx_ref, cbx_ref, cwb_ref, cbb_ref,
     dtb_ref, a_ref, ex_ref, h0_ref) = refs[:15]
    if final:
        dskip_ref, z_ref, yprev_ref, norm_ref, y_ref, hfin_ref, st_ref, ybuf_ref = refs[15:]
    else:
        y_ref, hfin_ref, st_ref = refs[15:]
        ybuf_ref = y_ref
    L = SSD_CHUNK
    heads = SSD_W // SSD_HEAD_DIM
    per_lane = LANE // SSD_HEAD_DIM
    n_pairs = heads // per_lane
    pairs_per_group = n_pairs // SSD_GROUPS
    c = pl.program_id(1)
    cc = (nc - 1 - c) if reverse else c

    @pl.when(c == 0)
    def _():
        st_ref[...] = h0_ref[...]

    xs = _conv_silu(xp_ref, xc_ref, xn_ref, cwx_ref, cbx_ref, cc > 0, cc < nc - 1)
    bcm = _conv_silu(bp_ref, bc_ref, bn_ref, cwb_ref, cbb_ref, cc > 0, cc < nc - 1)

    raw = dt_ref[...] + dtb_ref[...]
    dt = jnp.maximum(raw, 0.0) + jnp.log(1.0 + jnp.exp(-jnp.abs(raw)))
    ri = lax.broadcasted_iota(jnp.int32, (L, L), 0)
    ci = lax.broadcasted_iota(jnp.int32, (L, L), 1)
    causal = (ci >= ri) if reverse else (ci <= ri)
    tri = jnp.where(causal, 1.0, 0.0).astype(f32)
    acum = jnp.dot(tri, dt * a_ref[...], preferred_element_type=f32, precision=HIGHEST)
    acum_t = acum.T
    wide = jnp.dot(jnp.concatenate([acum, dt], axis=0), ex_ref[...], preferred_element_type=f32,
                   precision=HIGHEST)
    acum_x, dt_x = wide[:L], wide[L:]
    last = 0 if reverse else L - 1
    lane0 = heads if reverse else 0
    tot_x = acum_x[last:last + 1, :]
    xdt = xs * dt_x
    xdt_end = (xdt * jnp.exp(tot_x - acum_x)).astype(bf16)
    in_scale = jnp.exp(acum_x)
    st_scale = jnp.exp(tot_x)
    low_half = lax.broadcasted_iota(jnp.int32, (L, LANE), 1) < SSD_HEAD_DIM

    for g in range(SSD_GROUPS):
        bm_f = bcm[:, g * SSD_STATE:(g + 1) * SSD_STATE]
        bm = bm_f.astype(bf16)
        bm_t = bm_f.T.astype(bf16)
        cm_off = SSD_GROUPS * SSD_STATE
        cm = bcm[:, cm_off + g * SSD_STATE: cm_off + (g + 1) * SSD_STATE].astype(bf16)
        cb = lax.dot_general(cm, bm, (((1,), (1,)), ((), ())), preferred_element_type=f32)
        cb = jnp.where(causal, cb, 0.0)
        for pp in range(pairs_per_group):
            pr = g * pairs_per_group + pp
            cols = slice(pr * LANE, (pr + 1) * LANE)
            ws = []
            for hh in range(per_lane):
                ln = lane0 + pr * per_lane + hh
                seg = acum[:, ln:ln + 1] - acum_t[ln:ln + 1, :]
                ws.append((cb * jnp.exp(jnp.minimum(seg, 0.0))).astype(bf16))
            x2 = xdt[:, cols]
            xbd = jnp.concatenate([jnp.where(low_half, x2, 0.0).astype(bf16),
                                   jnp.where(low_half, 0.0, x2).astype(bf16)], axis=0)
            y = jnp.dot(jnp.concatenate(ws, axis=1), xbd, preferred_element_type=f32)
            st = st_ref[pr]
            y = y + jnp.dot(cm, st.astype(bf16), preferred_element_type=f32) * in_scale[:, cols]
            st_ref[pr] = st * st_scale[:, cols] + jnp.dot(bm_t, xdt_end[:, cols], preferred_element_type=f32)
            ybuf_ref[:, cols] = y

    @pl.when(c == nc - 1)
    def _():
        hfin_ref[...] = st_ref[...]

    if final:
        yt = (ybuf_ref[...] + yprev_ref[...] + dskip_ref[...] * xs) * _silu(z_ref[...])
        yn = yt * lax.rsqrt(jnp.mean(yt * yt, axis=-1, keepdims=True) + EPS) * norm_ref[...]
        y_ref[...] = yn.astype(y_ref.dtype)


def ssd_state_to_lanes(h):
    n, heads, P, N = h.shape
    per_lane = LANE // P
    return h.reshape(n, heads // per_lane, per_lane, P, N).transpose(0, 1, 4, 2, 3).reshape(
        n, heads // per_lane, N, per_lane * P)


def ssd_state_from_lanes(h):
    n, pairs, N, w = h.shape
    per_lane = w // SSD_HEAD_DIM
    return h.reshape(n, pairs, N, per_lane, SSD_HEAD_DIM).transpose(0, 1, 3, 4, 2).reshape(
        n, pairs * per_lane, SSD_HEAD_DIM, N)


def ssd_pass(proj, cols, prm, h0, n_seq, T, row0, reverse, y_prev=None):
    final = y_prev is not None
    L = SSD_CHUNK
    nc = T // L
    bc_w = 2 * SSD_GROUPS * SSD_STATE
    heads = SSD_W // SSD_HEAD_DIM
    assert LANE // SSD_HEAD_DIM == 2 and (heads // 2) % SSD_GROUPS == 0
    hb = L // SUBLANE
    c0 = row0 // L
    last_blk = (proj.shape[0] // SUBLANE) - 1

    def chunk(c):
        return (nc - 1 - c) if reverse else c

    def cur(w, col):
        return pl.BlockSpec((L, w), lambda s, c: (c0 + s * nc + chunk(c), col))

    def prev(w, col):
        return pl.BlockSpec((SUBLANE, w), lambda s, c: (jnp.maximum((c0 + s * nc + chunk(c)) * hb - 1, 0), col))

    def nxt(w, col):
        return pl.BlockSpec((SUBLANE, w),
                            lambda s, c: (jnp.minimum((c0 + s * nc + chunk(c) + 1) * hb, last_blk), col))

    out_rows = pl.BlockSpec((L, SSD_W), lambda s, c: (s * nc + chunk(c), 0))
    full = lambda a: pl.BlockSpec(a.shape, lambda s, c: (0,) * a.ndim)
    st_spec = pl.BlockSpec((None,) + h0.shape[1:], lambda s, c: (s, 0, 0, 0))
    d = 1 if reverse else 0
    consts = [prm["cw_x"], prm["cb_x"], prm["cw_bc"], prm["cb_bc"], prm["dt_bias"], prm["a_lane"], prm["expand"][d]]
    in_specs = ([prev(SSD_W, cols["xs"]), cur(SSD_W, cols["xs"]), nxt(SSD_W, cols["xs"]),
                 prev(bc_w, cols["bc"]), cur(bc_w, cols["bc"]), nxt(bc_w, cols["bc"]), cur(LANE, cols["dt"])]
                + [full(a) for a in consts] + [st_spec])
    args = [proj] * 7 + consts + [h0]
    scratch = [pltpu.VMEM(h0.shape[1:], f32)]
    if final:
        in_specs += [full(prm["dskip"]), cur(SSD_W, cols["z"]), out_rows, full(prm["norm"])]
        args += [prm["dskip"], proj, y_prev, prm["norm"]]
        scratch.append(pltpu.VMEM((L, SSD_W), f32))
    return pl.pallas_call(
        functools.partial(_ssd_kernel, reverse, final, nc),
        grid=(n_seq, nc),
        in_specs=in_specs,
        out_specs=[out_rows, st_spec],
        out_shape=[jax.ShapeDtypeStruct((n_seq * T, SSD_W), bf16 if final else f32),
                   jax.ShapeDtypeStruct(h0.shape, f32)],
        scratch_shapes=scratch,
        compiler_params=_cparams("parallel", "arbitrary"),
        name="ssd_final" if final else "ssd_first",
    )(*args)


def _hgrn_kernel(reverse, final, nblk, *refs):
    hq_ref, hf_ref, hi_ref, lb_ref, s0_ref = refs[:5]
    if final:
        hg_ref, oprev_ref, norm_ref, y_ref, sfin_ref, st_ref, q_s, k_s, bc_s, f_s, o_s = refs[5:]
    else:
        y_ref, sfin_ref, st_ref, q_s, k_s, bc_s, f_s = refs[5:]
        o_s = y_ref
    R = HGRN_ROWS
    C = HGRN_CHUNK
    nsub = R // C
    heads = HGRN_W // HGRN_DK
    DK = HGRN_DK
    b = pl.program_id(1)

    @pl.when(b == 0)
    def _():
        for h in range(heads):
            st_ref[h] = s0_ref[h].T

    r = hf_ref[...]
    lb = lb_ref[...]
    e = jnp.exp(-jnp.abs(r))
    inv = 1.0 / (1.0 + e)
    sig_pos = jnp.where(r >= 0, inv, e * inv)
    sig_neg = jnp.where(r >= 0, e * inv, inv)
    f = lb + (1.0 - lb) * sig_pos
    f_s[...] = f
    logf = jnp.log(f)
    k_s[...] = (1.0 - lb) * sig_neg
    q_s[...] = _silu(hq_ref[...])
    ri = lax.broadcasted_iota(jnp.int32, (R, R), 0)
    ci = lax.broadcasted_iota(jnp.int32, (R, R), 1)
    same = (ri // C) == (ci // C)
    tri = jnp.where(same & ((ci >= ri) if reverse else (ci <= ri)), 1.0, 0.0).astype(f32)
    bc_s[...] = jnp.dot(tri, logf, preferred_element_type=f32, precision=HIGHEST)

    ii = lax.broadcasted_iota(jnp.int32, (SUBLANE, DK), 0)
    last = 0 if reverse else C - 1

    for t in range(nsub):
        r0 = ((nsub - 1 - t) if reverse else t) * C
        rows = pl.ds(r0, C)
        for h in range(heads):
            hs = slice(h * DK, (h + 1) * DK)
            q = q_s[rows, hs]
            k = k_s[rows, hs]
            bc = bc_s[rows, hs]
            v = hi_ref[rows, hs]
            st = st_ref[h]
            o = lax.dot_general((q * jnp.exp(bc)).astype(bf16), st.astype(bf16), (((1,), (1,)), ((), ())),
                                preferred_element_type=f32)
            groups = [slice(gi * SUBLANE, (gi + 1) * SUBLANE) for gi in range(C // SUBLANE)]
            og = [o[s] for s in groups]
            qd = [None] * len(groups)
            for j in (range(C) if reverse else range(C - 1, -1, -1)):
                gj, jj = divmod(j, SUBLANE)
                step = j - 1 if reverse else j + 1
                k_row = k_s[pl.ds(r0 + j, 1), hs]
                v_row = hi_ref[pl.ds(r0 + j, 1), hs]
                f_row = f_s[pl.ds(r0 + step, 1), hs] if 0 <= step < C else None
                for gi in (range(gj + 1) if reverse else range(gj, len(groups))):
                    prev = None if qd[gi] is None else qd[gi] * f_row
                    if gi == gj:
                        qd[gi] = jnp.where(ii == jj, q[groups[gi]], 0.0 if prev is None else prev)
                    else:
                        qd[gi] = prev
                    a = jnp.sum(qd[gi] * k_row, axis=-1, keepdims=True)
                    og[gi] = og[gi] + a * v_row
            o_s[rows, hs] = jnp.concatenate(og, axis=0)
            bl = bc[last:last + 1, :]
            kd = (k * jnp.exp(bl - bc)).astype(bf16)
            upd = lax.dot_general(v.astype(bf16), kd, (((0,), (0,)), ((), ())), preferred_element_type=f32)
            st_ref[h] = st * jnp.exp(bl) + upd

    @pl.when(b == nblk - 1)
    def _():
        for h in range(heads):
            sfin_ref[h] = st_ref[h].T

    if final:
        ot = o_s[...] + oprev_ref[...]
        on = ot * lax.rsqrt(jnp.mean(ot * ot, axis=-1, keepdims=True) + EPS) * norm_ref[...]
        y_ref[...] = (on * _silu(hg_ref[...])).astype(y_ref.dtype)


def hgrn_pass(ph, f_col, lb, s0, n_seq, T, row0, reverse, o_prev=None, norm_g=None):
    final = o_prev is not None
    R = HGRN_ROWS
    nblk = T // R
    heads = HGRN_W // HGRN_DK
    b0 = row0 // R

    def blk(b):
        return (nblk - 1 - b) if reverse else b

    def col(j):
        return pl.BlockSpec((R, HGRN_W), lambda s, b: (b0 + s * nblk + blk(b), j))

    out_rows = pl.BlockSpec((R, HGRN_W), lambda s, b: (s * nblk + blk(b), 0))
    full = lambda a: pl.BlockSpec(a.shape, lambda s, b: (0,) * a.ndim)
    st_spec = pl.BlockSpec((None, heads, HGRN_DK, HGRN_DV), lambda s, b: (s, 0, 0, 0))
    in_specs = [col(0), col(f_col), col(3), full(lb), st_spec]
    args = [ph, ph, ph, lb, s0]
    scratch = [pltpu.VMEM((heads, HGRN_DV, HGRN_DK), f32)] + [pltpu.VMEM((R, HGRN_W), f32)] * 4
    if final:
        in_specs += [col(4), out_rows, full(norm_g)]
        args += [ph, o_prev, norm_g]
        scratch.append(pltpu.VMEM((R, HGRN_W), f32))
    return pl.pallas_call(
        functools.partial(_hgrn_kernel, reverse, final, nblk),
        grid=(n_seq, nblk),
        in_specs=in_specs,
        out_specs=[out_rows, st_spec],
        out_shape=[jax.ShapeDtypeStruct((n_seq * T, HGRN_W), bf16 if final else f32),
                   jax.ShapeDtypeStruct((n_seq, heads, HGRN_DK, HGRN_DV), f32)],
        scratch_shapes=scratch,
        compiler_params=_cparams("parallel", "arbitrary"),
        name="hgrn_final" if final else "hgrn_first",
    )(*args)


def _row_copy(src_hbm, dst, sem, src_row, dst_row):
    return pltpu.make_async_copy(src_hbm.at[pl.ds(src_row, 1), :], dst.at[pl.ds(dst_row, 1), :], sem)


def _moe_gather_kernel(nv_ref, idx_ref, idx_next_ref, h_hbm, o_ref, buf, sem):
    n = buf.shape[1]
    t = pl.program_id(0)
    n_valid = nv_ref[0]
    slot = t % 2

    def issue(idx, s):
        def start(r, c):
            _row_copy(h_hbm, buf.at[s], sem.at[s], idx[0, 0, r], r).start()
            return c

        lax.fori_loop(0, n, start, 0)

    @pl.when((t == 0) & (n_valid > 0))
    def _():
        issue(idx_ref, 0)

    @pl.when(t + 1 < n_valid)
    def _():
        issue(idx_next_ref, 1 - slot)

    @pl.when(t < n_valid)
    def _():
        def wait(r, c):
            _row_copy(h_hbm, buf.at[slot], sem.at[slot], 0, r).wait()
            return c

        lax.fori_loop(0, n, wait, 0)
        o_ref[...] = buf[slot].astype(o_ref.dtype)

    @pl.when(t >= n_valid)
    def _():
        o_ref[...] = jnp.zeros(o_ref.shape, o_ref.dtype)


def moe_gather(n_valid, h2, slot_tok):
    D = h2.shape[1]
    n_tiles, _, tm = slot_tok.shape
    return pl.pallas_call(
        _moe_gather_kernel,
        grid_spec=pltpu.PrefetchScalarGridSpec(
            num_scalar_prefetch=1, grid=(n_tiles,),
            in_specs=[pl.BlockSpec((1, 1, tm), lambda t, nv: (t, 0, 0), memory_space=pltpu.SMEM),
                      pl.BlockSpec((1, 1, tm), lambda t, nv: (jnp.minimum(t + 1, n_tiles - 1), 0, 0),
                                   memory_space=pltpu.SMEM),
                      pl.BlockSpec(memory_space=pl.ANY)],
            out_specs=pl.BlockSpec((tm, D), lambda t, nv: (t, 0)),
            scratch_shapes=[pltpu.VMEM((2, tm, D), h2.dtype), pltpu.SemaphoreType.DMA((2,))]),
        out_shape=jax.ShapeDtypeStruct((n_tiles * tm, D), bf16),
        compiler_params=_cparams("arbitrary"),
        name="moe_gather",
    )(n_valid, slot_tok, slot_tok, h2)


def _moe_up_kernel(te_ref, nv_ref, x_ref, wg_ref, wu_ref, o_ref):
    @pl.when(pl.program_id(0) < nv_ref[0])
    def _():
        x = x_ref[...]
        a = jnp.dot(x, wg_ref[...], preferred_element_type=f32)
        u = jnp.dot(x, wu_ref[...], preferred_element_type=f32)
        o_ref[...] = (_silu(a) * u).astype(o_ref.dtype)

    @pl.when(pl.program_id(0) >= nv_ref[0])
    def _():
        o_ref[...] = jnp.zeros(o_ref.shape, o_ref.dtype)


def moe_up(tile_e, n_valid, xs, wg, wu, layer):
    P, D = xs.shape
    F = wg.shape[3]
    tm = MOE_TILE
    wspec = pl.BlockSpec((None, None, D, F), lambda t, te, nv: (layer, te[t], 0, 0), pipeline_mode=pl.Buffered(1))
    return pl.pallas_call(
        _moe_up_kernel,
        grid_spec=pltpu.PrefetchScalarGridSpec(
            num_scalar_prefetch=2, grid=(P // tm,),
            in_specs=[pl.BlockSpec((tm, D), lambda t, te, nv: (t, 0)), wspec, wspec],
            out_specs=pl.BlockSpec((tm, F), lambda t, te, nv: (t, 0))),
        out_shape=jax.ShapeDtypeStruct((P, F), bf16),
        compiler_params=_cparams("arbitrary"),
        name="moe_up",
    )(tile_e, n_valid, xs, wg, wu)


def _moe_down_kernel(te_ref, nv_ref, a_ref, w_ref, o_ref):
    @pl.when(pl.program_id(0) < nv_ref[0])
    def _():
        o_ref[...] = jnp.dot(a_ref[...], w_ref[...], preferred_element_type=f32)

    @pl.when(pl.program_id(0) >= nv_ref[0])
    def _():
        o_ref[...] = jnp.zeros(o_ref.shape, o_ref.dtype)


def moe_down(tile_e, n_valid, act, wd, layer):
    P, F = act.shape
    D = wd.shape[3]
    tm = MOE_TILE
    return pl.pallas_call(
        _moe_down_kernel,
        grid_spec=pltpu.PrefetchScalarGridSpec(
            num_scalar_prefetch=2, grid=(P // tm,),
            in_specs=[pl.BlockSpec((tm, F), lambda t, te, nv: (t, 0)),
                      pl.BlockSpec((None, None, F, D), lambda t, te, nv: (layer, te[t], 0, 0),
                                   pipeline_mode=pl.Buffered(1))],
            out_specs=pl.BlockSpec((tm, D), lambda t, te, nv: (t, 0))),
        out_shape=jax.ShapeDtypeStruct((P, D), f32),
        compiler_params=_cparams("arbitrary"),
        name="moe_down",
    )(tile_e, n_valid, act, wd)


def _moe_combine_kernel(pos_ref, pos_next_ref, w_ref, x_ref, g_ref, ys_hbm, o_ref, buf, sem):
    n = x_ref.shape[0]
    i = pl.program_id(0)
    slot = i % 2

    def issue(pos, s):
        def start(r, c):
            for kk in range(TOP_K):
                _row_copy(ys_hbm, buf.at[s, kk], sem.at[s], pos[0, kk, r], r).start()
            return c

        lax.fori_loop(0, n, start, 0)

    @pl.when(i == 0)
    def _():
        issue(pos_ref, 0)

    @pl.when(i + 1 < pl.num_programs(0))
    def _():
        issue(pos_next_ref, 1 - slot)

    def wait(r, c):
        for kk in range(TOP_K):
            _row_copy(ys_hbm, buf.at[slot, kk], sem.at[slot], 0, r).wait()
        return c

    lax.fori_loop(0, n, wait, 0)
    w = w_ref[...]
    y = w[:, 0:1] * buf[slot, 0]
    for kk in range(1, TOP_K):
        y = y + w[:, kk:kk + 1] * buf[slot, kk]
    o_ref[...] = x_ref[...] + g_ref[...] * y


def moe_combine(x, gate, ys, pos, top_w):
    M, D = x.shape
    n_tiles, _, tm = pos.shape
    return pl.pallas_call(
        _moe_combine_kernel,
        grid=(n_tiles,),
        in_specs=[pl.BlockSpec((1, TOP_K, tm), lambda i: (i, 0, 0), memory_space=pltpu.SMEM),
                  pl.BlockSpec((1, TOP_K, tm), lambda i: (jnp.minimum(i + 1, n_tiles - 1), 0, 0),
                               memory_space=pltpu.SMEM),
                  pl.BlockSpec((tm, TOP_K), lambda i: (i, 0)),
                  pl.BlockSpec((tm, D), lambda i: (i, 0)),
                  pl.BlockSpec((None, 1, D), lambda i: (_group_of_tile(i, tm), 0, 0)),
                  pl.BlockSpec(memory_space=pl.ANY)],
        out_specs=pl.BlockSpec((tm, D), lambda i: (i, 0)),
        out_shape=jax.ShapeDtypeStruct((M, D), f32),
        scratch_shapes=[pltpu.VMEM((2, TOP_K, tm, D), f32), pltpu.SemaphoreType.DMA((2,))],
        compiler_params=_cparams("arbitrary"),
        name="moe_combine",
    )(pos, pos, top_w, x, gate, ys)


def moe_ffn(x, norm_g, sc, sh, gate, w_router, wg, wu, wd, layer):
    M, D = x.shape
    E = w_router.shape[1]
    wr = jnp.zeros((D, LANE), f32).at[:, :E].set(w_router)
    h2, logits = norm_router(x, norm_g, sc, sh, wr)
    top_v, top_i = lax.top_k(logits[:, :E], TOP_K)
    top_w = jax.nn.softmax(top_v, axis=-1)
    nk = M * TOP_K
    n_tiles = -(-nk // MOE_TILE) + E
    n_slots = n_tiles * MOE_TILE
    e_flat = top_i.reshape(-1).astype(jnp.int32)
    onehot = (e_flat[:, None] == jnp.arange(E, dtype=jnp.int32)[None, :]).astype(jnp.int32)
    running = jnp.cumsum(onehot, axis=0)
    rank = jnp.sum((running - onehot) * onehot, axis=1)
    counts = running[-1]
    padded = (counts + MOE_TILE - 1) // MOE_TILE * MOE_TILE
    start = jnp.cumsum(counts) - counts
    pend = jnp.cumsum(padded)
    pstart = pend - padded
    pos = jnp.sum(onehot * pstart[None, :], axis=1) + rank
    n_valid = (pend[-1:] // MOE_TILE).astype(jnp.int32)
    tile_e = jnp.minimum(jnp.searchsorted(pend, jnp.arange(n_tiles, dtype=jnp.int32) * MOE_TILE, side='right'),
                         E - 1).astype(jnp.int32)
    order = jnp.argsort(e_flat, stable=True).astype(jnp.int32)
    slot = jnp.arange(n_slots, dtype=jnp.int32)
    slot_e = jnp.repeat(tile_e, MOE_TILE)
    r = slot - pstart[slot_e]
    live = r < counts[slot_e]
    slot_tok = jnp.where(live, order[jnp.where(live, start[slot_e] + r, 0)] // TOP_K, slot % M)
    xs = moe_gather(n_valid, h2, slot_tok.reshape(n_tiles, 1, MOE_TILE))
    act = moe_up(tile_e, n_valid, xs, wg, wu, layer)
    ys = moe_down(tile_e, n_valid, act, wd, layer)
    tc = _row_tile((256, 128))
    pos_t = pos.reshape(M // tc, tc, TOP_K).transpose(0, 2, 1)
    return moe_combine(x, gate, ys, pos_t, top_w)


def _rope_tables():
    rows = DEC_SEQ // GRID_W
    row = jnp.repeat(jnp.arange(rows), GRID_W)
    col = jnp.tile(jnp.arange(GRID_W), rows)
    quarter = HEAD_DIM // 4
    inv = ROPE_THETA ** (-jnp.arange(quarter, dtype=f32) / quarter)
    ar = row.astype(f32)[:, None] * inv[None, :]
    ac = col.astype(f32)[:, None] * inv[None, :]
    cos = jnp.concatenate([jnp.cos(ar), jnp.cos(ar), jnp.cos(ac), jnp.cos(ac)], axis=-1)
    sin = jnp.concatenate([-jnp.sin(ar), jnp.sin(ar), -jnp.sin(ac), jnp.sin(ac)], axis=-1)
    rep = LANE // HEAD_DIM
    return jnp.tile(cos, (1, rep)), jnp.tile(sin, (1, rep))


def kernel(x_prompt, x_sample, cache_k, cache_v, state_ssd, state_hgrn, c, c_ctx, w_ada, b_ada, norm1_g, norm2_g, w_in, attn_sink, ssd_conv_w, ssd_conv_b, ssd_dt_bias, ssd_a_log, ssd_d, ssd_norm_g, hgrn_lb, hgrn_norm_g, w_gate, w_br_att, w_br_ssd, w_br_hgrn, w_out, ffn_wg, ffn_wu, ffn_wd, router_w, moe_wg, moe_wu, moe_wd, final_g):
    att_w, kv_w, ssd_heads, conv_ch, hgrn_heads, nc, nl = _dims()
    D = D_MODEL
    G = 1 + DEC_BATCH
    g_pad = -(-G // SUBLANE) * SUBLANE
    x = jnp.concatenate([x_prompt.reshape(nc, D), x_sample.reshape(nl, D)], axis=0)
    m = jnp.zeros((g_pad, D), f32).at[0].set(c_ctx).at[1:G].set(c)
    mods = ada_mod(m, w_ada, b_ada)
    sm = jax.nn.softmax(hgrn_lb.astype(f32), axis=1)
    lb_all = jnp.cumsum(sm, axis=1) - sm[:, :1]
    cos, sin = _rope_tables()

    cuts = np.cumsum([0, att_w, kv_w, kv_w, SSD_W, conv_ch, 2 * ssd_heads, 5 * HGRN_W, GATE_RANK])
    bc_w = conv_ch - SSD_W
    off, o = {}, 0
    for name, w in (("hgrn", 5 * HGRN_W), ("q", att_w), ("z", SSD_W), ("xs", SSD_W), ("bc", bc_w),
                    ("kv", 2 * kv_w), ("gc", GATE_RANK), ("dt", LANE)):
        off[name] = o
        o += w

    def blk(name, width):
        assert off[name] % width == 0
        return off[name] // width

    ssd_cols = {"xs": blk("xs", SSD_W), "bc": blk("bc", bc_w), "z": blk("z", SSD_W), "dt": blk("dt", LANE)}
    expand = np.zeros((2, LANE, SSD_W), np.float32)
    for d in range(2):
        for hh in range(ssd_heads):
            expand[d, d * ssd_heads + hh, hh * SSD_HEAD_DIM:(hh + 1) * SSD_HEAD_DIM] = 1.0
    expand = jnp.asarray(expand)
    lanes = jnp.zeros((1, LANE), f32)
    zero_ssd = jnp.zeros((BATCH, ssd_heads * SSD_HEAD_DIM // LANE, SSD_STATE, LANE), f32)
    zero_h = jnp.zeros((BATCH, hgrn_heads, HGRN_DK, HGRN_DV), f32)

    wb = {"gate": w_gate, "br_att": w_br_att, "br_ssd": w_br_ssd, "br_hgrn": w_br_hgrn, "out": w_out,
          "ffn_g": ffn_wg, "ffn_u": ffn_wu, "ffn_d": ffn_wd, "moe_g": moe_wg, "moe_u": moe_wu, "moe_d": moe_wd}
    wb = {name: cast_bf16(w) for name, w in wb.items()}

    ks, vs, hs, ss = [], [], [], []
    for l in range(DEPTH):
        mod = mods[l].reshape(g_pad, 6, 1, D)
        sh1, sc1, g1, sh2, sc2, g2 = [mod[:, j] for j in range(6)]
        wl = w_in[l]
        w_proj = jnp.concatenate(
            [wl[:, cuts[6]:cuts[7]], wl[:, cuts[0]:cuts[1]], wl[:, cuts[3]:cuts[4]],
             wl[:, cuts[4]:cuts[4] + SSD_W], wl[:, cuts[4] + SSD_W:cuts[5]], wl[:, cuts[1]:cuts[3]],
             wl[:, cuts[7]:cuts[8]], wl[:, cuts[5]:cuts[6]], jnp.zeros((D, DT_PAD - 2 * ssd_heads), f32)],
            axis=1).astype(bf16)
        proj = norm_matmul(x, norm1_g[l], sc1, sh1, w_proj)

        y_att_c = attention_context(proj, blk("q", att_w), blk("kv", kv_w), attn_sink[l])
        q_rot, k_rot, vt = rope_latent(proj, blk("q", att_w), blk("kv", 2 * kv_w), cos, sin, nc)
        y_att_l = attention_latent(
            q_rot, k_rot, vt, cache_k[:, l].reshape(DEC_BATCH, PAST_LEN, kv_w).astype(bf16),
            cache_v[:, l].reshape(DEC_BATCH, PAST_LEN, kv_w).transpose(0, 2, 1).astype(bf16), attn_sink[l])
        y_att = (y_att_c, y_att_l)
        ks.append(proj[:nc, off["kv"]:off["kv"] + kv_w].reshape(BATCH, SEQ, N_KV, HEAD_DIM))
        vs.append(proj[:nc, off["kv"] + kv_w:off["kv"] + 2 * kv_w].reshape(BATCH, SEQ, N_KV, HEAD_DIM))

        prm = {
            "cw_x": ssd_conv_w[l][:, :SSD_W], "cb_x": ssd_conv_b[l][:SSD_W].reshape(1, SSD_W),
            "cw_bc": ssd_conv_w[l][:, SSD_W:], "cb_bc": ssd_conv_b[l][SSD_W:].reshape(1, bc_w),
            "dt_bias": lanes.at[0, :2 * ssd_heads].set(ssd_dt_bias[l].reshape(-1)),
            "a_lane": lanes.at[0, :2 * ssd_heads].set(-jnp.exp(ssd_a_log[l].astype(f32)).reshape(-1)),
            "expand": expand,
            "dskip": jnp.repeat(ssd_d[l].astype(f32), SSD_HEAD_DIM).reshape(1, SSD_W),
            "norm": ssd_norm_g[l].reshape(1, SSD_W),
        }
        y_ssd, h_ssd = [], []
        for (n_seq, T, row0, h0f, h0b) in (
                (BATCH, SEQ, 0, zero_ssd, zero_ssd),
                (DEC_BATCH, DEC_SEQ, nc, ssd_state_to_lanes(state_ssd[:, l, 0]), ssd_state_to_lanes(state_ssd[:, l, 1]))):
            yf, hf = ssd_pass(proj, ssd_cols, prm, h0f, n_seq, T, row0, False)
            yb, hb = ssd_pass(proj, ssd_cols, prm, h0b, n_seq, T, row0, True, y_prev=yf)
            y_ssd.append(yb)
            h_ssd.append((hf, hb))
        hs.append(jnp.stack([ssd_state_from_lanes(t) for t in h_ssd[0]], axis=1))

        assert off["hgrn"] == 0
        lb = lb_all[:, l]
        hn = hgrn_norm_g[l].reshape(1, HGRN_W)
        y_hgrn, s_hgrn = [], []
        for (n_seq, T, row0, s0f, s0b) in ((BATCH, SEQ, 0, zero_h, zero_h),
                                           (DEC_BATCH, DEC_SEQ, nc, state_hgrn[:, l, 0], state_hgrn[:, l, 1])):
            of, sf = hgrn_pass(proj, 1, lb[0:1], s0f, n_seq, T, row0, False)
            ob, sb = hgrn_pass(proj, 2, lb[1:2], s0b, n_seq, T, row0, True, o_prev=of, norm_g=hn)
            y_hgrn.append(ob)
            s_hgrn.append((sf, sb))
        ss.append(jnp.stack(s_hgrn[0], axis=1))

        merged = merge_branches(proj, blk("gc", GATE_RANK), y_att, y_ssd, y_hgrn, wb["gate"], wb["br_att"],
                                wb["br_ssd"], wb["br_hgrn"], l)
        x = matmul_residual(merged, wb["out"], l, x, g1, "out_proj")

        i = l // 2
        if l % 2 == 0:
            act = norm_glu_up(x, norm2_g[l], sc2, sh2, wb["ffn_g"], wb["ffn_u"], i)
            x = matmul_residual(act, wb["ffn_d"], i, x, g2, "ffn_down")
        else:
            x = moe_ffn(x, norm2_g[l], sc2, sh2, g2, router_w[i], wb["moe_g"], wb["moe_u"], wb["moe_d"], i)

    y_prompt = final_norm(x, final_g, 0, nc).reshape(BATCH, SEQ, D)
    y_sample = final_norm(x, final_g, nc, nl).reshape(DEC_BATCH, DEC_SEQ, D)
    return (y_prompt, y_sample, jnp.stack(ks, axis=1), jnp.stack(vs, axis=1),
            jnp.stack(hs, axis=1), jnp.stack(ss, axis=1))
```

```python
import functools

import numpy as np
import jax
import jax.numpy as jnp
from jax import lax
from jax.experimental import pallas as pl
from jax.experimental.pallas import tpu as pltpu

f32 = jnp.float32
bf16 = jnp.bfloat16

D_MODEL = 4096
BATCH = 16
SEQ = 256
DEPTH = 4
DEC_BATCH = 4
DEC_SEQ = 4096
PAST_LEN = 256
GRID_W = 64
N_HEADS = 16
N_KV = 4
HEAD_DIM = 64
WINDOW = 128
BLOCK = 128
ROPE_THETA = 10000.0
SSD_W = 1024
SSD_HEAD_DIM = 64
SSD_GROUPS = 2
SSD_STATE = 128
SSD_CONV = 5
SSD_CHUNK = 128
HGRN_W = 1024
HGRN_DK = 128
HGRN_DV = 128
HGRN_CHUNK = 16
GATE_RANK = 512
D_FF = 5632
N_EXPERTS = 8
TOP_K = 2
EXPERT_FF = 1408
EPS = 1e-6

LANE = 128
SUBLANE = 8
VMEM_LIMIT = 56 * 1024 * 1024
MOE_TILE = 512
HGRN_ROWS = 128
DT_PAD = 2 * LANE
HIGHEST = lax.Precision.HIGHEST


def _cparams(*sem):
    return pltpu.CompilerParams(dimension_semantics=sem, vmem_limit_bytes=VMEM_LIMIT)


def _pick(n, prefs):
    for p in prefs:
        if n % p == 0:
            return p
    return n


def _silu(x):
    return x * (1.0 / (1.0 + jnp.exp(-x)))


def _sigmoid(x):
    return 1.0 / (1.0 + jnp.exp(-x))


def _dims():
    att_w = N_HEADS * HEAD_DIM
    kv_w = N_KV * HEAD_DIM
    ssd_heads = SSD_W // SSD_HEAD_DIM
    conv_ch = SSD_W + 2 * SSD_GROUPS * SSD_STATE
    hgrn_heads = HGRN_W // HGRN_DK
    nc = BATCH * SEQ
    nl = DEC_BATCH * DEC_SEQ
    return att_w, kv_w, ssd_heads, conv_ch, hgrn_heads, nc, nl


def _group_of_tile(i, tm):
    nc = BATCH * SEQ
    r = i * tm
    return jnp.where(r < nc, 0, 1 + (r - nc) // DEC_SEQ)


def _row_tile(prefs=(512, 256, 128)):
    nc = BATCH * SEQ
    return _pick(int(np.gcd(nc, DEC_SEQ)), prefs)


def _ada_kernel(m_ref, w_ref, b_ref, o_ref):
    a = _silu(m_ref[...]).astype(bf16)
    o_ref[...] = jnp.dot(a, w_ref[...].astype(bf16), preferred_element_type=f32) + b_ref[...]


def ada_mod(m_pad, w_ada, b_ada):
    L, D, N = w_ada.shape
    G = m_pad.shape[0]
    tn = _pick(N, (512, 256, 128))
    return pl.pallas_call(
        _ada_kernel,
        grid=(L, N // tn),
        in_specs=[pl.BlockSpec((G, D), lambda l, j: (0, 0)),
                  pl.BlockSpec((None, D, tn), lambda l, j: (l, 0, j)),
                  pl.BlockSpec((None, 1, tn), lambda l, j: (l, 0, j))],
        out_specs=pl.BlockSpec((None, G, tn), lambda l, j: (l, 0, j)),
        out_shape=jax.ShapeDtypeStruct((L, G, N), f32),
        compiler_params=_cparams("parallel", "parallel"),
        name="ada_mod",
    )(m_pad, w_ada, b_ada.reshape(L, 1, N))


def _rms_mod(x, g, sc=None, sh=None):
    y = x * lax.rsqrt(jnp.mean(x * x, axis=-1, keepdims=True) + EPS) * g
    if sc is not None:
        y = y * (1.0 + sc) + sh
    return y


def _norm_kernel(x_ref, g_ref, o_ref):
    o_ref[...] = _rms_mod(x_ref[...], g_ref[...]).astype(o_ref.dtype)


def final_norm(x, g, row0, n_rows):
    D = x.shape[1]
    tm = _row_tile((256, 128))
    off = row0 // tm
    return pl.pallas_call(
        _norm_kernel,
        grid=(n_rows // tm,),
        in_specs=[pl.BlockSpec((tm, D), lambda i: (i + off, 0)), pl.BlockSpec((1, D), lambda i: (0, 0))],
        out_specs=pl.BlockSpec((tm, D), lambda i: (i, 0)),
        out_shape=jax.ShapeDtypeStruct((n_rows, D), f32),
        compiler_params=_cparams("parallel"),
        name="final_norm",
    )(x, g.reshape(1, D))


def _cast_kernel(x_ref, o_ref):
    o_ref[...] = x_ref[...].astype(o_ref.dtype)


def cast_bf16(w):
    shape = w.shape
    w2 = w.reshape(-1, shape[-1])
    R, C = w2.shape
    tr = _pick(R, (1024, 512, 256, 128, 64, 32, 16))
    tc = _pick(C, (2048, 1408, 1024, 512, 256, 128))
    out = pl.pallas_call(
        _cast_kernel,
        grid=(R // tr, C // tc),
        in_specs=[pl.BlockSpec((tr, tc), lambda i, j: (i, j))],
        out_specs=pl.BlockSpec((tr, tc), lambda i, j: (i, j)),
        out_shape=jax.ShapeDtypeStruct((R, C), bf16),
        compiler_params=_cparams("parallel", "parallel"),
        name="cast_bf16",
    )(w2)
    return out.reshape(shape)


def _norm_router_kernel(x_ref, g_ref, sc_ref, sh_ref, wr_ref, h_ref, lg_ref):
    h = _rms_mod(x_ref[...], g_ref[...], sc_ref[...], sh_ref[...])
    h_ref[...] = h
    lg_ref[...] = jnp.dot(h, wr_ref[...], preferred_element_type=f32, precision=HIGHEST)


def norm_router(x, g, sc, sh, w_router_pad):
    M, D = x.shape
    tm = _row_tile((256, 128))
    row = pl.BlockSpec((tm, D), lambda i: (i, 0))
    grp = pl.BlockSpec((None, 1, D), lambda i: (_group_of_tile(i, tm), 0, 0))
    return pl.pallas_call(
        _norm_router_kernel,
        grid=(M // tm,),
        in_specs=[row, pl.BlockSpec((1, D), lambda i: (0, 0)), grp, grp,
                  pl.BlockSpec((D, LANE), lambda i: (0, 0))],
        out_specs=[row, pl.BlockSpec((tm, LANE), lambda i: (i, 0))],
        out_shape=[jax.ShapeDtypeStruct((M, D), f32), jax.ShapeDtypeStruct((M, LANE), f32)],
        compiler_params=_cparams("parallel"),
        name="norm_router",
    )(x, g.reshape(1, D), sc, sh, w_router_pad)


class _NormPipe:
    def __init__(self, M, D, nj):
        self.tm = _row_tile((1024, 512, 256, 128))
        self.n_tiles = M // self.tm
        self.chunks = max(c for c in (8, 4, 2, 1) if c <= nj and self.tm % (c * SUBLANE) == 0)
        self.rows = self.tm // self.chunks
        self.D = D
        self.nj = nj

    def tile(self, it):
        return jnp.minimum(it, self.n_tiles - 1)

    def in_specs(self):
        D, tm = self.D, self.tm
        grp = pl.BlockSpec((None, 1, D), lambda it, j: (_group_of_tile(self.tile(it), tm), 0, 0))
        x_chunk = pl.BlockSpec(
            (self.rows, D), lambda it, j: (self.tile(it) * self.chunks + jnp.minimum(j, self.chunks - 1), 0))
        return [x_chunk, pl.BlockSpec((1, D), lambda it, j: (0, 0)), grp, grp]

    def out_spec(self, tn):
        return pl.BlockSpec((self.tm, tn), lambda it, j: (jnp.where(it == 0, self.n_tiles, it - 1), j))

    def out_rows(self):
        return (self.n_tiles + 1) * self.tm

    def scratch(self):
        return [pltpu.VMEM((self.tm, self.D), bf16), pltpu.VMEM((self.tm, self.D), bf16)]

    def grid(self):
        return (self.n_tiles + 1, self.nj)


def _norm_pipelined(chunks, product, x_ref, g_ref, sc_ref, sh_ref, o_ref, h_even, h_odd):
    it, j = pl.program_id(0), pl.program_id(1)
    rows = x_ref.shape[0]
    r0 = pl.multiple_of(jnp.minimum(j, chunks - 1) * rows, rows)

    def step(fill, ready):
        fill[pl.ds(r0, rows), :] = _rms_mod(x_ref[...], g_ref[...], sc_ref[...], sh_ref[...]).astype(bf16)
        if ready is None:
            o_ref[...] = jnp.zeros(o_ref.shape, o_ref.dtype)
        else:
            o_ref[...] = product(ready[...]).astype(o_ref.dtype)

    @pl.when(it == 0)
    def _():
        step(h_even, None)

    @pl.when((it > 0) & (it % 2 == 0))
    def _():
        step(h_even, h_odd)

    @pl.when(it % 2 == 1)
    def _():
        step(h_odd, h_even)


def _norm_mm_kernel(chunks, x_ref, g_ref, sc_ref, sh_ref, w_ref, o_ref, h_even, h_odd):
    def product(h):
        return jnp.dot(h, w_ref[...], preferred_element_type=f32)

    _norm_pipelined(chunks, product, x_ref, g_ref, sc_ref, sh_ref, o_ref, h_even, h_odd)


def norm_matmul(x, g, sc, sh, w):
    M, D = x.shape
    N = w.shape[1]
    tn = _pick(N, (768, 1024, 512, 256, 128))
    plan = _NormPipe(M, D, N // tn)
    return pl.pallas_call(
        functools.partial(_norm_mm_kernel, plan.chunks),
        grid=plan.grid(),
        in_specs=plan.in_specs() + [pl.BlockSpec((D, tn), lambda it, j: (0, j))],
        out_specs=plan.out_spec(tn),
        out_shape=jax.ShapeDtypeStruct((plan.out_rows(), N), f32),
        scratch_shapes=plan.scratch(),
        compiler_params=_cparams("arbitrary", "arbitrary"),
        name="in_proj",
    )(x, g.reshape(1, D), sc, sh, w)


def _norm_glu_kernel(chunks, x_ref, g_ref, sc_ref, sh_ref, wg_ref, wu_ref, o_ref, h_even, h_odd):
    def product(h):
        a = jnp.dot(h, wg_ref[...], preferred_element_type=f32)
        u = jnp.dot(h, wu_ref[...], preferred_element_type=f32)
        return _silu(a) * u

    _norm_pipelined(chunks, product, x_ref, g_ref, sc_ref, sh_ref, o_ref, h_even, h_odd)


def norm_glu_up(x, g, sc, sh, wg, wu, layer):
    M, D = x.shape
    N = wg.shape[2]
    tn = _pick(N, (512, 256, 128))
    plan = _NormPipe(M, D, N // tn)
    wspec = pl.BlockSpec((None, D, tn), lambda it, j: (layer, 0, j))
    return pl.pallas_call(
        functools.partial(_norm_glu_kernel, plan.chunks),
        grid=plan.grid(),
        in_specs=plan.in_specs() + [wspec, wspec],
        out_specs=plan.out_spec(tn),
        out_shape=jax.ShapeDtypeStruct((plan.out_rows(), N), bf16),
        scratch_shapes=plan.scratch(),
        compiler_params=_cparams("arbitrary", "arbitrary"),
        name="glu_up",
    )(x, g.reshape(1, D), sc, sh, wg, wu)


def _mm_res_kernel(a_ref, w_ref, x_ref, g_ref, o_ref):
    y = jnp.dot(a_ref[...], w_ref[...], preferred_element_type=f32)
    o_ref[...] = x_ref[...] + g_ref[...] * y


def matmul_residual(a, w, layer, x, gate, name="matmul_residual"):
    M, K = x.shape[0], a.shape[1]
    N = w.shape[2]
    tm = _row_tile((512, 256, 128))
    tn = _pick(N, (1024, 512, 256, 128))
    return pl.pallas_call(
        _mm_res_kernel,
        grid=(N // tn, M // tm),
        in_specs=[pl.BlockSpec((tm, K), lambda j, i: (i, 0)),
                  pl.BlockSpec((None, K, tn), lambda j, i: (layer, 0, j)),
                  pl.BlockSpec((tm, tn), lambda j, i: (i, j)),
                  pl.BlockSpec((None, 1, tn), lambda j, i: (_group_of_tile(i, tm), 0, j))],
        out_specs=pl.BlockSpec((tm, tn), lambda j, i: (i, j)),
        out_shape=jax.ShapeDtypeStruct((M, N), f32),
        compiler_params=_cparams("parallel", "parallel"),
        name=name,
    )(a, w, x, gate)


def _merge_kernel(n_ctx_tiles, gc_ref, yac_ref, yal_ref, ysc_ref, ysl_ref, yhc_ref, yhl_ref,
                  wga_ref, wgs_ref, wgh_ref, wa_ref, ws_ref, wh_ref, o_ref):
    gc = gc_ref[...].astype(bf16)
    is_ctx = pl.program_id(1) < n_ctx_tiles

    def branch(yc_ref, yl_ref, wg_ref, w_ref):
        y = jnp.where(is_ctx, yc_ref[...], yl_ref[...])
        gate = _sigmoid(jnp.dot(gc, wg_ref[...], preferred_element_type=f32))
        return gate * jnp.dot(y, w_ref[...], preferred_element_type=f32)

    o_ref[...] = (branch(yac_ref, yal_ref, wga_ref, wa_ref) + branch(ysc_ref, ysl_ref, wgs_ref, ws_ref)
                  + branch(yhc_ref, yhl_ref, wgh_ref, wh_ref)).astype(o_ref.dtype)


def merge_branches(proj, gc_col, y_att, y_ssd, y_hgrn, w_gate, w_att, w_ssd, w_hgrn, layer):
    M = y_att[0].shape[0] + y_att[1].shape[0]
    R = w_gate.shape[1]
    D = w_att.shape[2]
    tm = _row_tile((512, 256, 128))
    tn = _pick(D, (1024, 512, 256, 128))
    nj = D // tn
    n_ctx = (BATCH * SEQ) // tm
    n_lat = (DEC_BATCH * DEC_SEQ) // tm

    def pair(y):
        w = y[0].shape[1]
        return [pl.BlockSpec((tm, w), lambda j, i: (jnp.minimum(i, n_ctx - 1), 0)),
                pl.BlockSpec((tm, w), lambda j, i: (jnp.clip(i - n_ctx, 0, n_lat - 1), 0))]

    def gate_cols(b):
        return pl.BlockSpec((None, R, tn), lambda j, i: (layer, 0, j + b * nj))

    def cols(w):
        return pl.BlockSpec((None, w.shape[1], tn), lambda j, i: (layer, 0, j))

    return pl.pallas_call(
        functools.partial(_merge_kernel, n_ctx),
        grid=(nj, M // tm),
        in_specs=[pl.BlockSpec((tm, R), lambda j, i: (i, gc_col))] + pair(y_att) + pair(y_ssd) + pair(y_hgrn)
                 + [gate_cols(0), gate_cols(1), gate_cols(2), cols(w_att), cols(w_ssd), cols(w_hgrn)],
        out_specs=pl.BlockSpec((tm, tn), lambda j, i: (i, j)),
        out_shape=jax.ShapeDtypeStruct((M, D), bf16),
        compiler_params=_cparams("parallel", "parallel"),
        name="merge_branches",
    )(proj, *y_att, *y_ssd, *y_hgrn, w_gate, w_gate, w_gate, w_att, w_ssd, w_hgrn)


def _softmax_parts(parts, sink):
    m = sink
    for s in parts:
        m = jnp.maximum(m, jnp.max(s, axis=-1, keepdims=True))
    ps = [jnp.exp(s - m) for s in parts]
    den = jnp.exp(sink - m)
    for p in ps:
        den = den + jnp.sum(p, axis=-1, keepdims=True)
    return ps, 1.0 / den


def _attend_group(sink_ref, kv, q_heads, pieces, o_ref):
    g = len(q_heads)
    outs = []
    for gi, q in enumerate(q_heads):
        scores = []
        for k, _, bias in pieces:
            s = lax.dot_general(q, k, (((1,), (1,)), ((), ())), preferred_element_type=f32)
            scores.append(s if bias is None else s + bias)
        ps, inv = _softmax_parts(scores, sink_ref[kv * g + gi])
        o = jnp.dot(ps[0].astype(bf16), pieces[0][1], preferred_element_type=f32)
        for p, (_, v, _) in zip(ps[1:], pieces[1:]):
            o = o + jnp.dot(p.astype(bf16), v, preferred_element_type=f32)
        outs.append(o * inv)
    per_lane = LANE // HEAD_DIM
    for pi in range(g // per_lane):
        tile = jnp.concatenate(outs[pi * per_lane:(pi + 1) * per_lane], axis=1)
        c0 = (kv * g + pi * per_lane) * HEAD_DIM
        o_ref[:, c0:c0 + LANE] = tile.astype(o_ref.dtype)


def _attn_ctx_kernel(sink_ref, q_ref, k_ref, v_ref, o_ref):
    g = N_HEADS // N_KV
    scale = HEAD_DIM ** -0.5
    for kv in range(N_KV):
        sl = slice(kv * HEAD_DIM, (kv + 1) * HEAD_DIM)
        k = k_ref[:, sl].astype(bf16)
        v = v_ref[:, sl].astype(bf16)
        qs = [(q_ref[:, (kv * g + gi) * HEAD_DIM:(kv * g + gi + 1) * HEAD_DIM] * scale).astype(bf16)
              for gi in range(g)]
        _attend_group(sink_ref, kv, qs, [(k, v, None)], o_ref)


def attention_context(proj, q_col, k_col, sink):
    att_w, kv_w = N_HEADS * HEAD_DIM, N_KV * HEAD_DIM
    return pl.pallas_call(
        _attn_ctx_kernel,
        grid=(BATCH,),
        in_specs=[pl.BlockSpec(memory_space=pltpu.SMEM),
                  pl.BlockSpec((SEQ, att_w), lambda b: (b, q_col)),
                  pl.BlockSpec((SEQ, kv_w), lambda b: (b, k_col)),
                  pl.BlockSpec((SEQ, kv_w), lambda b: (b, k_col + 1))],
        out_specs=pl.BlockSpec((SEQ, att_w), lambda b: (b, 0)),
        out_shape=jax.ShapeDtypeStruct((BATCH * SEQ, att_w), bf16),
        compiler_params=_cparams("parallel"),
        name="attention_context",
    )(sink, proj, proj, proj)


def _rope_kernel(n_q, n_k, q_ref, kv_ref, cos_ref, sin_ref, qo_ref, ko_ref, vt_ref):
    cos = cos_ref[...]
    sin = sin_ref[...]
    quarter = HEAD_DIM // 4
    lane = lax.broadcasted_iota(jnp.int32, cos.shape, 1)
    first = (lane % (2 * quarter)) < quarter

    def rot(x):
        swapped = jnp.where(first, pltpu.roll(x, LANE - quarter, axis=1), pltpu.roll(x, quarter, axis=1))
        return x * cos + swapped * sin

    scale = HEAD_DIM ** -0.5
    for j in range(n_q):
        sl = slice(j * LANE, (j + 1) * LANE)
        qo_ref[:, sl] = (rot(q_ref[:, sl]) * scale).astype(qo_ref.dtype)
    for j in range(n_k):
        sl = slice(j * LANE, (j + 1) * LANE)
        ko_ref[:, sl] = rot(kv_ref[:, sl]).astype(ko_ref.dtype)
    vt_ref[...] = kv_ref[:, n_k * LANE:].T.astype(vt_ref.dtype)


def rope_latent(proj, q_col, kv_col, cos, sin, row0):
    att_w, kv_w = N_HEADS * HEAD_DIM, N_KV * HEAD_DIM
    nl = DEC_BATCH * DEC_SEQ
    tm = _pick(DEC_SEQ, (512, 256, 128))
    off = row0 // tm
    per_seq = DEC_SEQ // tm
    tab = pl.BlockSpec((tm, LANE), lambda i: (i % per_seq, 0))
    return pl.pallas_call(
        functools.partial(_rope_kernel, att_w // LANE, kv_w // LANE),
        grid=(nl // tm,),
        in_specs=[pl.BlockSpec((tm, att_w), lambda i: (i + off, q_col)),
                  pl.BlockSpec((tm, 2 * kv_w), lambda i: (i + off, kv_col)), tab, tab],
        out_specs=[pl.BlockSpec((tm, att_w), lambda i: (i, 0)), pl.BlockSpec((tm, kv_w), lambda i: (i, 0)),
                   pl.BlockSpec((kv_w, tm), lambda i: (0, i))],
        out_shape=[jax.ShapeDtypeStruct((nl, att_w), bf16), jax.ShapeDtypeStruct((nl, kv_w), bf16),
                   jax.ShapeDtypeStruct((kv_w, nl), bf16)],
        compiler_params=_cparams("parallel"),
        name="rope_latent",
    )(proj, proj, cos, sin)


def _attn_lat_kernel(nb, sink_ref, q_ref, kp_ref, kc_ref, kn_ref, vtp_ref, vtc_ref, vtn_ref, kctx_ref, vtctx_ref,
                     o_ref):
    n = pl.program_id(1)
    g = N_HEADS // N_KV
    kj = lax.broadcasted_iota(jnp.int32, (3 * BLOCK, BLOCK), 0)
    qi = lax.broadcasted_iota(jnp.int32, (3 * BLOCK, BLOCK), 1)
    valid = (jnp.abs(kj - BLOCK - qi) <= WINDOW)
    valid = valid & ((kj >= BLOCK) | (n > 0)) & ((kj < 2 * BLOCK) | (n < nb - 1))
    bias = jnp.where(valid, 0.0, -1e30).astype(f32)
    k_loc_all = jnp.concatenate([kp_ref[...], kc_ref[...], kn_ref[...]], axis=0)
    vt_loc_all = jnp.concatenate([vtp_ref[...], vtc_ref[...], vtn_ref[...]], axis=1)
    nt = (((1,), (1,)), ((), ()))
    per_lane = LANE // HEAD_DIM
    for kv in range(N_KV):
        sl = slice(kv * HEAD_DIM, (kv + 1) * HEAD_DIM)
        k_loc, k_ctx = k_loc_all[:, sl], kctx_ref[:, sl]
        vt_loc, vt_ctx = vt_loc_all[sl, :], vtctx_ref[sl, :]
        outs = []
        for gi in range(g):
            h = kv * g + gi
            q = q_ref[:, h * HEAD_DIM:(h + 1) * HEAD_DIM]
            s_ctx = lax.dot_general(k_ctx, q, nt, preferred_element_type=f32)
            s_loc = lax.dot_general(k_loc, q, nt, preferred_element_type=f32) + bias
            sink = sink_ref[h]
            m = jnp.maximum(jnp.maximum(jnp.max(s_ctx, axis=0, keepdims=True),
                                        jnp.max(s_loc, axis=0, keepdims=True)), sink)
            p_ctx = jnp.exp(s_ctx - m)
            p_loc = jnp.exp(s_loc - m)
            den = (jnp.sum(p_ctx, axis=0, keepdims=True) + jnp.sum(p_loc, axis=0, keepdims=True)
                   + jnp.exp(sink - m))
            o_t = (jnp.dot(vt_ctx, p_ctx.astype(bf16), preferred_element_type=f32)
                   + jnp.dot(vt_loc, p_loc.astype(bf16), preferred_element_type=f32))
            outs.append(o_t * (1.0 / den))
        for pi in range(g // per_lane):
            tile = jnp.concatenate(outs[pi * per_lane:(pi + 1) * per_lane], axis=0).T
            c0 = (kv * g + pi * per_lane) * HEAD_DIM
            o_ref[:, c0:c0 + LANE] = tile.astype(o_ref.dtype)


def attention_latent(q_rot, k_rot, vt, k_ctx, vt_ctx, sink):
    att_w, kv_w = q_rot.shape[1], k_rot.shape[1]
    nb = DEC_SEQ // BLOCK
    past = k_ctx.shape[1]
    prev = lambda n: jnp.maximum(n - 1, 0)
    same = lambda n: n
    nxt = lambda n: jnp.minimum(n + 1, nb - 1)
    kspec = lambda f: pl.BlockSpec((BLOCK, kv_w), lambda b, n: (b * nb + f(n), 0))
    vspec = lambda f: pl.BlockSpec((kv_w, BLOCK), lambda b, n: (0, b * nb + f(n)))
    return pl.pallas_call(
        functools.partial(_attn_lat_kernel, nb),
        grid=(DEC_BATCH, nb),
        in_specs=[pl.BlockSpec(memory_space=pltpu.SMEM),
                  pl.BlockSpec((BLOCK, att_w), lambda b, n: (b * nb + n, 0)),
                  kspec(prev), kspec(same), kspec(nxt), vspec(prev), vspec(same), vspec(nxt),
                  pl.BlockSpec((None, past, kv_w), lambda b, n: (b, 0, 0)),
                  pl.BlockSpec((None, kv_w, past), lambda b, n: (b, 0, 0))],
        out_specs=pl.BlockSpec((BLOCK, att_w), lambda b, n: (b * nb + n, 0)),
        out_shape=jax.ShapeDtypeStruct((DEC_BATCH * DEC_SEQ, att_w), bf16),
        compiler_params=_cparams("parallel", "parallel"),
        name="attention_latent",
    )(sink, q_rot, k_rot, k_rot, k_rot, vt, vt, vt, k_ctx, vt_ctx)


def _conv_silu(p_ref, c_ref, n_ref, w_ref, b_ref, has_prev, has_next):
    L = SSD_CHUNK
    lo = SUBLANE - SSD_CONV // 2
    xp = jnp.where(has_prev, p_ref[...], 0.0)
    xn = jnp.where(has_next, n_ref[...], 0.0)
    ext = jnp.concatenate([xp, c_ref[...], xn], axis=0)
    acc = b_ref[...] + ext[lo: lo + L] * w_ref[0:1, :]
    for k in range(1, SSD_CONV):
        acc = acc + ext[lo + k: lo + k + L] * w_ref[k:k + 1, :]
    return _silu(acc)


def _ssd_kernel(reverse, final, nc, *refs):
    (xp_ref, xc_ref, xn_ref, bp_ref, bc_ref, bn_ref, dt_ref, cwx_ref, cbx_ref, cwb_ref, cbb_ref,
     dtb_ref, a_ref, ex_ref, h0_ref) = refs[:15]
    if final:
        dskip_ref, z_ref, yprev_ref, norm_ref, y_ref, hfin_ref, st_ref, ybuf_ref = refs[15:]
    else:
        y_ref, hfin_ref, st_ref = refs[15:]
        ybuf_ref = y_ref
    L = SSD_CHUNK
    heads = SSD_W // SSD_HEAD_DIM
    per_lane = LANE // SSD_HEAD_DIM
    n_pairs = heads // per_lane
    pairs_per_group = n_pairs // SSD_GROUPS
    c = pl.program_id(1)
    cc = (nc - 1 - c) if reverse else c

    @pl.when(c == 0)
    def _():
        st_ref[...] = h0_ref[...]

    xs = _conv_silu(xp_ref, xc_ref, xn_ref, cwx_ref, cbx_ref, cc > 0, cc < nc - 1)
    bcm = _conv_silu(bp_ref, bc_ref, bn_ref, cwb_ref, cbb_ref, cc > 0, cc < nc - 1)

    raw = dt_ref[...] + dtb_ref[...]
    dt = jnp.maximum(raw, 0.0) + jnp.log(1.0 + jnp.exp(-jnp.abs(raw)))
    ri = lax.broadcasted_iota(jnp.int32, (L, L), 0)
    ci = lax.broadcasted_iota(jnp.int32, (L, L), 1)
    causal = (ci >= ri) if reverse else (ci <= ri)
    tri = jnp.where(causal, 1.0, 0.0).astype(f32)
    acum = jnp.dot(tri, dt * a_ref[...], preferred_element_type=f32, precision=HIGHEST)
    acum_t = acum.T
    wide = jnp.dot(jnp.concatenate([acum, dt], axis=0), ex_ref[...], preferred_element_type=f32,
                   precision=HIGHEST)
    acum_x, dt_x = wide[:L], wide[L:]
    last = 0 if reverse else L - 1
    lane0 = heads if reverse else 0
    tot_x = acum_x[last:last + 1, :]
    xdt = xs * dt_x
    xdt_end = (xdt * jnp.exp(tot_x - acum_x)).astype(bf16)
    in_scale = jnp.exp(acum_x)
    st_scale = jnp.exp(tot_x)
    low_half = lax.broadcasted_iota(jnp.int32, (L, LANE), 1) < SSD_HEAD_DIM

    for g in range(SSD_GROUPS):
        bm_f = bcm[:, g * SSD_STATE:(g + 1) * SSD_STATE]
        bm = bm_f.astype(bf16)
        bm_t = bm_f.T.astype(bf16)
        cm_off = SSD_GROUPS * SSD_STATE
        cm = bcm[:, cm_off + g * SSD_STATE: cm_off + (g + 1) * SSD_STATE].astype(bf16)
        cb = lax.dot_general(cm, bm, (((1,), (1,)), ((), ())), preferred_element_type=f32)
        cb = jnp.where(causal, cb, 0.0)
        for pp in range(pairs_per_group):
            pr = g * pairs_per_group + pp
            cols = slice(pr * LANE, (pr + 1) * LANE)
            ws = []
            for hh in range(per_lane):
                ln = lane0 + pr * per_lane + hh
                seg = acum[:, ln:ln + 1] - acum_t[ln:ln + 1, :]
                ws.append((cb * jnp.exp(jnp.minimum(seg, 0.0))).astype(bf16))
            x2 = xdt[:, cols]
            xbd = jnp.concatenate([jnp.where(low_half, x2, 0.0).astype(bf16),
                                   jnp.where(low_half, 0.0, x2).astype(bf16)], axis=0)
            y = jnp.dot(jnp.concatenate(ws, axis=1), xbd, preferred_element_type=f32)
            st = st_ref[pr]
            y = y + jnp.dot(cm, st.astype(bf16), preferred_element_type=f32) * in_scale[:, cols]
            st_ref[pr] = st * st_scale[:, cols] + jnp.dot(bm_t, xdt_end[:, cols], preferred_element_type=f32)
            ybuf_ref[:, cols] = y

    @pl.when(c == nc - 1)
    def _():
        hfin_ref[...] = st_ref[...]

    if final:
        yt = (ybuf_ref[...] + yprev_ref[...] + dskip_ref[...] * xs) * _silu(z_ref[...])
        yn = yt * lax.rsqrt(jnp.mean(yt * yt, axis=-1, keepdims=True) + EPS) * norm_ref[...]
        y_ref[...] = yn.astype(y_ref.dtype)


def ssd_state_to_lanes(h):
    n, heads, P, N = h.shape
    per_lane = LANE // P
    return h.reshape(n, heads // per_lane, per_lane, P, N).transpose(0, 1, 4, 2, 3).reshape(
        n, heads // per_lane, N, per_lane * P)


def ssd_state_from_lanes(h):
    n, pairs, N, w = h.shape
    per_lane = w // SSD_HEAD_DIM
    return h.reshape(n, pairs, N, per_lane, SSD_HEAD_DIM).transpose(0, 1, 3, 4, 2).reshape(
        n, pairs * per_lane, SSD_HEAD_DIM, N)


def ssd_pass(proj, cols, prm, h0, n_seq, T, row0, reverse, y_prev=None):
    final = y_prev is not None
    L = SSD_CHUNK
    nc = T // L
    bc_w = 2 * SSD_GROUPS * SSD_STATE
    heads = SSD_W // SSD_HEAD_DIM
    assert LANE // SSD_HEAD_DIM == 2 and (heads // 2) % SSD_GROUPS == 0
    hb = L // SUBLANE
    c0 = row0 // L
    last_blk = (proj.shape[0] // SUBLANE) - 1

    def chunk(c):
        return (nc - 1 - c) if reverse else c

    def cur(w, col):
        return pl.BlockSpec((L, w), lambda s, c: (c0 + s * nc + chunk(c), col))

    def prev(w, col):
        return pl.BlockSpec((SUBLANE, w), lambda s, c: (jnp.maximum((c0 + s * nc + chunk(c)) * hb - 1, 0), col))

    def nxt(w, col):
        return pl.BlockSpec((SUBLANE, w),
                            lambda s, c: (jnp.minimum((c0 + s * nc + chunk(c) + 1) * hb, last_blk), col))

    out_rows = pl.BlockSpec((L, SSD_W), lambda s, c: (s * nc + chunk(c), 0))
    full = lambda a: pl.BlockSpec(a.shape, lambda s, c: (0,) * a.ndim)
    st_spec = pl.BlockSpec((None,) + h0.shape[1:], lambda s, c: (s, 0, 0, 0))
    d = 1 if reverse else 0
    consts = [prm["cw_x"], prm["cb_x"], prm["cw_bc"], prm["cb_bc"], prm["dt_bias"], prm["a_lane"], prm["expand"][d]]
    in_specs = ([prev(SSD_W, cols["xs"]), cur(SSD_W, cols["xs"]), nxt(SSD_W, cols["xs"]),
                 prev(bc_w, cols["bc"]), cur(bc_w, cols["bc"]), nxt(bc_w, cols["bc"]), cur(LANE, cols["dt"])]
                + [full(a) for a in consts] + [st_spec])
    args = [proj] * 7 + consts + [h0]
    scratch = [pltpu.VMEM(h0.shape[1:], f32)]
    if final:
        in_specs += [full(prm["dskip"]), cur(SSD_W, cols["z"]), out_rows, full(prm["norm"])]
        args += [prm["dskip"], proj, y_prev, prm["norm"]]
        scratch.append(pltpu.VMEM((L, SSD_W), f32))
    return pl.pallas_call(
        functools.partial(_ssd_kernel, reverse, final, nc),
        grid=(n_seq, nc),
        in_specs=in_specs,
        out_specs=[out_rows, st_spec],
        out_shape=[jax.ShapeDtypeStruct((n_seq * T, SSD_W), bf16 if final else f32),
                   jax.ShapeDtypeStruct(h0.shape, f32)],
        scratch_shapes=scratch,
        compiler_params=_cparams("parallel", "arbitrary"),
        name="ssd_final" if final else "ssd_first",
    )(*args)


def _hgrn_kernel(reverse, final, nblk, *refs):
    hq_ref, hf_ref, hi_ref, lb_ref, s0_ref = refs[:5]
    if final:
        hg_ref, oprev_ref, norm_ref, y_ref, sfin_ref, st_ref, q_s, k_s, bc_s, f_s, o_s = refs[5:]
    else:
        y_ref, sfin_ref, st_ref, q_s, k_s, bc_s, f_s = refs[5:]
        o_s = y_ref
    R = HGRN_ROWS
    C = HGRN_CHUNK
    nsub = R // C
    heads = HGRN_W // HGRN_DK
    DK = HGRN_DK
    b = pl.program_id(1)

    @pl.when(b == 0)
    def _():
        for h in range(heads):
            st_ref[h] = s0_ref[h].T

    r = hf_ref[...]
    lb = lb_ref[...]
    e = jnp.exp(-jnp.abs(r))
    inv = 1.0 / (1.0 + e)
    sig_pos = jnp.where(r >= 0, inv, e * inv)
    sig_neg = jnp.where(r >= 0, e * inv, inv)
    f = lb + (1.0 - lb) * sig_pos
    f_s[...] = f
    logf = jnp.log(f)
    k_s[...] = (1.0 - lb) * sig_neg
    q_s[...] = _silu(hq_ref[...])
    ri = lax.broadcasted_iota(jnp.int32, (R, R), 0)
    ci = lax.broadcasted_iota(jnp.int32, (R, R), 1)
    same = (ri // C) == (ci // C)
    tri = jnp.where(same & ((ci >= ri) if reverse else (ci <= ri)), 1.0, 0.0).astype(f32)
    bc_s[...] = jnp.dot(tri, logf, preferred_element_type=f32, precision=HIGHEST)

    ii = lax.broadcasted_iota(jnp.int32, (SUBLANE, DK), 0)
    last = 0 if reverse else C - 1

    for t in range(nsub):
        r0 = ((nsub - 1 - t) if reverse else t) * C
        rows = pl.ds(r0, C)
        for h in range(heads):
            hs = slice(h * DK, (h + 1) * DK)
            q = q_s[rows, hs]
            k = k_s[rows, hs]
            bc = bc_s[rows, hs]
            v = hi_ref[rows, hs]
            st = st_ref[h]
            o = lax.dot_general((q * jnp.exp(bc)).astype(bf16), st.astype(bf16), (((1,), (1,)), ((), ())),
                                preferred_element_type=f32)
            groups = [slice(gi * SUBLANE, (gi + 1) * SUBLANE) for gi in range(C // SUBLANE)]
            og = [o[s] for s in groups]
            qd = [None] * len(groups)
            for j in (range(C) if reverse else range(C - 1, -1, -1)):
                gj, jj = divmod(j, SUBLANE)
                step = j - 1 if reverse else j + 1
                k_row = k_s[pl.ds(r0 + j, 1), hs]
                v_row = hi_ref[pl.ds(r0 + j, 1), hs]
                f_row = f_s[pl.ds(r0 + step, 1), hs] if 0 <= step < C else None
                for gi in (range(gj + 1) if reverse else range(gj, len(groups))):
                    prev = None if qd[gi] is None else qd[gi] * f_row
                    if gi == gj:
                        qd[gi] = jnp.where(ii == jj, q[groups[gi]], 0.0 if prev is None else prev)
                    else:
                        qd[gi] = prev
                    a = jnp.sum(qd[gi] * k_row, axis=-1, keepdims=True)
                    og[gi] = og[gi] + a * v_row
            o_s[rows, hs] = jnp.concatenate(og, axis=0)
            bl = bc[last:last + 1, :]
            kd = (k * jnp.exp(bl - bc)).astype(bf16)
            upd = lax.dot_general(v.astype(bf16), kd, (((0,), (0,)), ((), ())), preferred_element_type=f32)
            st_ref[h] = st * jnp.exp(bl) + upd

    @pl.when(b == nblk - 1)
    def _():
        for h in range(heads):
            sfin_ref[h] = st_ref[h].T

    if final:
        ot = o_s[...] + oprev_ref[...]
        on = ot * lax.rsqrt(jnp.mean(ot * ot, axis=-1, keepdims=True) + EPS) * norm_ref[...]
        y_ref[...] = (on * _silu(hg_ref[...])).astype(y_ref.dtype)


def hgrn_pass(ph, f_col, lb, s0, n_seq, T, row0, reverse, o_prev=None, norm_g=None):
    final = o_prev is not None
    R = HGRN_ROWS
    nblk = T // R
    heads = HGRN_W // HGRN_DK
    b0 = row0 // R

    def blk(b):
        return (nblk - 1 - b) if reverse else b

    def col(j):
        return pl.BlockSpec((R, HGRN_W), lambda s, b: (b0 + s * nblk + blk(b), j))

    out_rows = pl.BlockSpec((R, HGRN_W), lambda s, b: (s * nblk + blk(b), 0))
    full = lambda a: pl.BlockSpec(a.shape, lambda s, b: (0,) * a.ndim)
    st_spec = pl.BlockSpec((None, heads, HGRN_DK, HGRN_DV), lambda s, b: (s, 0, 0, 0))
    in_specs = [col(0), col(f_col), col(3), full(lb), st_spec]
    args = [ph, ph, ph, lb, s0]
    scratch = [pltpu.VMEM((heads, HGRN_DV, HGRN_DK), f32)] + [pltpu.VMEM((R, HGRN_W), f32)] * 4
    if final:
        in_specs += [col(4), out_rows, full(norm_g)]
        args += [ph, o_prev, norm_g]
        scratch.append(pltpu.VMEM((R, HGRN_W), f32))
    return pl.pallas_call(
        functools.partial(_hgrn_kernel, reverse, final, nblk),
        grid=(n_seq, nblk),
        in_specs=in_specs,
        out_specs=[out_rows, st_spec],
        out_shape=[jax.ShapeDtypeStruct((n_seq * T, HGRN_W), bf16 if final else f32),
                   jax.ShapeDtypeStruct((n_seq, heads, HGRN_DK, HGRN_DV), f32)],
        scratch_shapes=scratch,
        compiler_params=_cparams("parallel", "arbitrary"),
        name="hgrn_final" if final else "hgrn_first",
    )(*args)


def _row_copy(src_hbm, dst, sem, src_row, dst_row):
    return pltpu.make_async_copy(src_hbm.at[pl.ds(src_row, 1), :], dst.at[pl.ds(dst_row, 1), :], sem)


def _moe_gather_kernel(nv_ref, idx_ref, idx_next_ref, h_hbm, o_ref, buf, sem):
    n = buf.shape[1]
    t = pl.program_id(0)
    n_valid = nv_ref[0]
    slot = t % 2

    def issue(idx, s):
        def start(r, c):
            _row_copy(h_hbm, buf.at[s], sem.at[s], idx[0, 0, r], r).start()
            return c

        lax.fori_loop(0, n, start, 0)

    @pl.when((t == 0) & (n_valid > 0))
    def _():
        issue(idx_ref, 0)

    @pl.when(t + 1 < n_valid)
    def _():
        issue(idx_next_ref, 1 - slot)

    @pl.when(t < n_valid)
    def _():
        def wait(r, c):
            _row_copy(h_hbm, buf.at[slot], sem.at[slot], 0, r).wait()
            return c

        lax.fori_loop(0, n, wait, 0)
        o_ref[...] = buf[slot].astype(o_ref.dtype)

    @pl.when(t >= n_valid)
    def _():
        o_ref[...] = jnp.zeros(o_ref.shape, o_ref.dtype)


def moe_gather(n_valid, h2, slot_tok):
    D = h2.shape[1]
    n_tiles, _, tm = slot_tok.shape
    return pl.pallas_call(
        _moe_gather_kernel,
        grid_spec=pltpu.PrefetchScalarGridSpec(
            num_scalar_prefetch=1, grid=(n_tiles,),
            in_specs=[pl.BlockSpec((1, 1, tm), lambda t, nv: (t, 0, 0), memory_space=pltpu.SMEM),
                      pl.BlockSpec((1, 1, tm), lambda t, nv: (jnp.minimum(t + 1, n_tiles - 1), 0, 0),
                                   memory_space=pltpu.SMEM),
                      pl.BlockSpec(memory_space=pl.ANY)],
            out_specs=pl.BlockSpec((tm, D), lambda t, nv: (t, 0)),
            scratch_shapes=[pltpu.VMEM((2, tm, D), h2.dtype), pltpu.SemaphoreType.DMA((2,))]),
        out_shape=jax.ShapeDtypeStruct((n_tiles * tm, D), bf16),
        compiler_params=_cparams("arbitrary"),
        name="moe_gather",
    )(n_valid, slot_tok, slot_tok, h2)


def _moe_up_kernel(te_ref, nv_ref, x_ref, wg_ref, wu_ref, o_ref):
    @pl.when(pl.program_id(0) < nv_ref[0])
    def _():
        x = x_ref[...]
        a = jnp.dot(x, wg_ref[...], preferred_element_type=f32)
        u = jnp.dot(x, wu_ref[...], preferred_element_type=f32)
        o_ref[...] = (_silu(a) * u).astype(o_ref.dtype)

    @pl.when(pl.program_id(0) >= nv_ref[0])
    def _():
        o_ref[...] = jnp.zeros(o_ref.shape, o_ref.dtype)


def moe_up(tile_e, n_valid, xs, wg, wu, layer):
    P, D = xs.shape
    F = wg.shape[3]
    tm = MOE_TILE
    wspec = pl.BlockSpec((None, None, D, F), lambda t, te, nv: (layer, te[t], 0, 0), pipeline_mode=pl.Buffered(1))
    return pl.pallas_call(
        _moe_up_kernel,
        grid_spec=pltpu.PrefetchScalarGridSpec(
            num_scalar_prefetch=2, grid=(P // tm,),
            in_specs=[pl.BlockSpec((tm, D), lambda t, te, nv: (t, 0)), wspec, wspec],
            out_specs=pl.BlockSpec((tm, F), lambda t, te, nv: (t, 0))),
        out_shape=jax.ShapeDtypeStruct((P, F), bf16),
        compiler_params=_cparams("arbitrary"),
        name="moe_up",
    )(tile_e, n_valid, xs, wg, wu)


def _moe_down_kernel(te_ref, nv_ref, a_ref, w_ref, o_ref):
    @pl.when(pl.program_id(0) < nv_ref[0])
    def _():
        o_ref[...] = jnp.dot(a_ref[...], w_ref[...], preferred_element_type=f32)

    @pl.when(pl.program_id(0) >= nv_ref[0])
    def _():
        o_ref[...] = jnp.zeros(o_ref.shape, o_ref.dtype)


def moe_down(tile_e, n_valid, act, wd, layer):
    P, F = act.shape
    D = wd.shape[3]
    tm = MOE_TILE
    return pl.pallas_call(
        _moe_down_kernel,
        grid_spec=pltpu.PrefetchScalarGridSpec(
            num_scalar_prefetch=2, grid=(P // tm,),
            in_specs=[pl.BlockSpec((tm, F), lambda t, te, nv: (t, 0)),
                      pl.BlockSpec((None, None, F, D), lambda t, te, nv: (layer, te[t], 0, 0),
                                   pipeline_mode=pl.Buffered(1))],
            out_specs=pl.BlockSpec((tm, D), lambda t, te, nv: (t, 0))),
        out_shape=jax.ShapeDtypeStruct((P, D), f32),
        compiler_params=_cparams("arbitrary"),
        name="moe_down",
    )(tile_e, n_valid, act, wd)


def _moe_combine_kernel(pos_ref, pos_next_ref, w_ref, x_ref, g_ref, ys_hbm, o_ref, buf, sem):
    n = x_ref.shape[0]
    i = pl.program_id(0)
    slot = i % 2

    def issue(pos, s):
        def start(r, c):
            for kk in range(TOP_K):
                _row_copy(ys_hbm, buf.at[s, kk], sem.at[s], pos[0, kk, r], r).start()
            return c

        lax.fori_loop(0, n, start, 0)

    @pl.when(i == 0)
    def _():
        issue(pos_ref, 0)

    @pl.when(i + 1 < pl.num_programs(0))
    def _():
        issue(pos_next_ref, 1 - slot)

    def wait(r, c):
        for kk in range(TOP_K):
            _row_copy(ys_hbm, buf.at[slot, kk], sem.at[slot], 0, r).wait()
        return c

    lax.fori_loop(0, n, wait, 0)
    w = w_ref[...]
    y = w[:, 0:1] * buf[slot, 0]
    for kk in range(1, TOP_K):
        y = y + w[:, kk:kk + 1] * buf[slot, kk]
    o_ref[...] = x_ref[...] + g_ref[...] * y


def moe_combine(x, gate, ys, pos, top_w):
    M, D = x.shape
    n_tiles, _, tm = pos.shape
    return pl.pallas_call(
        _moe_combine_kernel,
        grid=(n_tiles,),
        in_specs=[pl.BlockSpec((1, TOP_K, tm), lambda i: (i, 0, 0), memory_space=pltpu.SMEM),
                  pl.BlockSpec((1, TOP_K, tm), lambda i: (jnp.minimum(i + 1, n_tiles - 1), 0, 0),
                               memory_space=pltpu.SMEM),
                  pl.BlockSpec((tm, TOP_K), lambda i: (i, 0)),
                  pl.BlockSpec((tm, D), lambda i: (i, 0)),
                  pl.BlockSpec((None, 1, D), lambda i: (_group_of_tile(i, tm), 0, 0)),
                  pl.BlockSpec(memory_space=pl.ANY)],
        out_specs=pl.BlockSpec((tm, D), lambda i: (i, 0)),
        out_shape=jax.ShapeDtypeStruct((M, D), f32),
        scratch_shapes=[pltpu.VMEM((2, TOP_K, tm, D), f32), pltpu.SemaphoreType.DMA((2,))],
        compiler_params=_cparams("arbitrary"),
        name="moe_combine",
    )(pos, pos, top_w, x, gate, ys)


def moe_ffn(x, norm_g, sc, sh, gate, w_router, wg, wu, wd, layer):
    M, D = x.shape
    E = w_router.shape[1]
    wr = jnp.zeros((D, LANE), f32).at[:, :E].set(w_router)
    h2, logits = norm_router(x, norm_g, sc, sh, wr)
    top_v, top_i = lax.top_k(logits[:, :E], TOP_K)
    top_w = jax.nn.softmax(top_v, axis=-1)
    nk = M * TOP_K
    n_tiles = -(-nk // MOE_TILE) + E
    n_slots = n_tiles * MOE_TILE
    e_flat = top_i.reshape(-1).astype(jnp.int32)
    onehot = (e_flat[:, None] == jnp.arange(E, dtype=jnp.int32)[None, :]).astype(jnp.int32)
    running = jnp.cumsum(onehot, axis=0)
    rank = jnp.sum((running - onehot) * onehot, axis=1)
    counts = running[-1]
    padded = (counts + MOE_TILE - 1) // MOE_TILE * MOE_TILE
    start = jnp.cumsum(counts) - counts
    pend = jnp.cumsum(padded)
    pstart = pend - padded
    pos = jnp.sum(onehot * pstart[None, :], axis=1) + rank
    n_valid = (pend[-1:] // MOE_TILE).astype(jnp.int32)
    tile_e = jnp.minimum(jnp.searchsorted(pend, jnp.arange(n_tiles, dtype=jnp.int32) * MOE_TILE, side='right'),
                         E - 1).astype(jnp.int32)
    order = jnp.argsort(e_flat, stable=True).astype(jnp.int32)
    slot = jnp.arange(n_slots, dtype=jnp.int32)
    slot_e = jnp.repeat(tile_e, MOE_TILE)
    r = slot - pstart[slot_e]
    live = r < counts[slot_e]
    slot_tok = jnp.where(live, order[jnp.where(live, start[slot_e] + r, 0)] // TOP_K, slot % M)
    xs = moe_gather(n_valid, h2, slot_tok.reshape(n_tiles, 1, MOE_TILE))
    act = moe_up(tile_e, n_valid, xs, wg, wu, layer)
    ys = moe_down(tile_e, n_valid, act, wd, layer)
    tc = _row_tile((256, 128))
    pos_t = pos.reshape(M // tc, tc, TOP_K).transpose(0, 2, 1)
    return moe_combine(x, gate, ys, pos_t, top_w)


def _rope_tables():
    rows = DEC_SEQ // GRID_W
    row = jnp.repeat(jnp.arange(rows), GRID_W)
    col = jnp.tile(jnp.arange(GRID_W), rows)
    quarter = HEAD_DIM // 4
    inv = ROPE_THETA ** (-jnp.arange(quarter, dtype=f32) / quarter)
    ar = row.astype(f32)[:, None] * inv[None, :]
    ac = col.astype(f32)[:, None] * inv[None, :]
    cos = jnp.concatenate([jnp.cos(ar), jnp.cos(ar), jnp.cos(ac), jnp.cos(ac)], axis=-1)
    sin = jnp.concatenate([-jnp.sin(ar), jnp.sin(ar), -jnp.sin(ac), jnp.sin(ac)], axis=-1)
    rep = LANE // HEAD_DIM
    return jnp.tile(cos, (1, rep)), jnp.tile(sin, (1, rep))


def kernel(x_prompt, x_sample, cache_k, cache_v, state_ssd, state_hgrn, c, c_ctx, w_ada, b_ada, norm1_g, norm2_g, w_in, attn_sink, ssd_conv_w, ssd_conv_b, ssd_dt_bias, ssd_a_log, ssd_d, ssd_norm_g, hgrn_lb, hgrn_norm_g, w_gate, w_br_att, w_br_ssd, w_br_hgrn, w_out, ffn_wg, ffn_wu, ffn_wd, router_w, moe_wg, moe_wu, moe_wd, final_g):
    att_w, kv_w, ssd_heads, conv_ch, hgrn_heads, nc, nl = _dims()
    D = D_MODEL
    G = 1 + DEC_BATCH
    g_pad = -(-G // SUBLANE) * SUBLANE
    x = jnp.concatenate([x_prompt.reshape(nc, D), x_sample.reshape(nl, D)], axis=0)
    m = jnp.zeros((g_pad, D), f32).at[0].set(c_ctx).at[1:G].set(c)
    mods = ada_mod(m, w_ada, b_ada)
    sm = jax.nn.softmax(hgrn_lb.astype(f32), axis=1)
    lb_all = jnp.cumsum(sm, axis=1) - sm[:, :1]
    cos, sin = _rope_tables()

    cuts = np.cumsum([0, att_w, kv_w, kv_w, SSD_W, conv_ch, 2 * ssd_heads, 5 * HGRN_W, GATE_RANK])
    bc_w = conv_ch - SSD_W
    off, o = {}, 0
    for name, w in (("hgrn", 5 * HGRN_W), ("q", att_w), ("z", SSD_W), ("xs", SSD_W), ("bc", bc_w),
                    ("kv", 2 * kv_w), ("gc", GATE_RANK), ("dt", LANE)):
        off[name] = o
        o += w

    def blk(name, width):
        assert off[name] % width == 0
        return off[name] // width

    ssd_cols = {"xs": blk("xs", SSD_W), "bc": blk("bc", bc_w), "z": blk("z", SSD_W), "dt": blk("dt", LANE)}
    expand = np.zeros((2, LANE, SSD_W), np.float32)
    for d in range(2):
        for hh in range(ssd_heads):
            expand[d, d * ssd_heads + hh, hh * SSD_HEAD_DIM:(hh + 1) * SSD_HEAD_DIM] = 1.0
    expand = jnp.asarray(expand)
    lanes = jnp.zeros((1, LANE), f32)
    zero_ssd = jnp.zeros((BATCH, ssd_heads * SSD_HEAD_DIM // LANE, SSD_STATE, LANE), f32)
    zero_h = jnp.zeros((BATCH, hgrn_heads, HGRN_DK, HGRN_DV), f32)

    wb = {"gate": w_gate, "br_att": w_br_att, "br_ssd": w_br_ssd, "br_hgrn": w_br_hgrn, "out": w_out,
          "ffn_g": ffn_wg, "ffn_u": ffn_wu, "ffn_d": ffn_wd, "moe_g": moe_wg, "moe_u": moe_wu, "moe_d": moe_wd}
    wb = {name: cast_bf16(w) for name, w in wb.items()}

    ks, vs, hs, ss = [], [], [], []
    for l in range(DEPTH):
        mod = mods[l].reshape(g_pad, 6, 1, D)
        sh1, sc1, g1, sh2, sc2, g2 = [mod[:, j] for j in range(6)]
        wl = w_in[l]
        w_proj = jnp.concatenate(
            [wl[:, cuts[6]:cuts[7]], wl[:, cuts[0]:cuts[1]], wl[:, cuts[3]:cuts[4]],
             wl[:, cuts[4]:cuts[4] + SSD_W], wl[:, cuts[4] + SSD_W:cuts[5]], wl[:, cuts[1]:cuts[3]],
             wl[:, cuts[7]:cuts[8]], wl[:, cuts[5]:cuts[6]], jnp.zeros((D, DT_PAD - 2 * ssd_heads), f32)],
            axis=1).astype(bf16)
        proj = norm_matmul(x, norm1_g[l], sc1, sh1, w_proj)

        y_att_c = attention_context(proj, blk("q", att_w), blk("kv", kv_w), attn_sink[l])
        q_rot, k_rot, vt = rope_latent(proj, blk("q", att_w), blk("kv", 2 * kv_w), cos, sin, nc)
        y_att_l = attention_latent(
            q_rot, k_rot, vt, cache_k[:, l].reshape(DEC_BATCH, PAST_LEN, kv_w).astype(bf16),
            cache_v[:, l].reshape(DEC_BATCH, PAST_LEN, kv_w).transpose(0, 2, 1).astype(bf16), attn_sink[l])
        y_att = (y_att_c, y_att_l)
        ks.append(proj[:nc, off["kv"]:off["kv"] + kv_w].reshape(BATCH, SEQ, N_KV, HEAD_DIM))
        vs.append(proj[:nc, off["kv"] + kv_w:off["kv"] + 2 * kv_w].reshape(BATCH, SEQ, N_KV, HEAD_DIM))

        prm = {
            "cw_x": ssd_conv_w[l][:, :SSD_W], "cb_x": ssd_conv_b[l][:SSD_W].reshape(1, SSD_W),
            "cw_bc": ssd_conv_w[l][:, SSD_W:], "cb_bc": ssd_conv_b[l][SSD_W:].reshape(1, bc_w),
            "dt_bias": lanes.at[0, :2 * ssd_heads].set(ssd_dt_bias[l].reshape(-1)),
            "a_lane": lanes.at[0, :2 * ssd_heads].set(-jnp.exp(ssd_a_log[l].astype(f32)).reshape(-1)),
            "expand": expand,
            "dskip": jnp.repeat(ssd_d[l].astype(f32), SSD_HEAD_DIM).reshape(1, SSD_W),
            "norm": ssd_norm_g[l].reshape(1, SSD_W),
        }
        y_ssd, h_ssd = [], []
        for (n_seq, T, row0, h0f, h0b) in (
                (BATCH, SEQ, 0, zero_ssd, zero_ssd),
                (DEC_BATCH, DEC_SEQ, nc, ssd_state_to_lanes(state_ssd[:, l, 0]), ssd_state_to_lanes(state_ssd[:, l, 1]))):
            yf, hf = ssd_pass(proj, ssd_cols, prm, h0f, n_seq, T, row0, False)
            yb, hb = ssd_pass(proj, ssd_cols, prm, h0b, n_seq, T, row0, True, y_prev=yf)
            y_ssd.append(yb)
            h_ssd.append((hf, hb))
        hs.append(jnp.stack([ssd_state_from_lanes(t) for t in h_ssd[0]], axis=1))

        assert off["hgrn"] == 0
        lb = lb_all[:, l]
        hn = hgrn_norm_g[l].reshape(1, HGRN_W)
        y_hgrn, s_hgrn = [], []
        for (n_seq, T, row0, s0f, s0b) in ((BATCH, SEQ, 0, zero_h, zero_h),
                                           (DEC_BATCH, DEC_SEQ, nc, state_hgrn[:, l, 0], state_hgrn[:, l, 1])):
            of, sf = hgrn_pass(proj, 1, lb[0:1], s0f, n_seq, T, row0, False)
            ob, sb = hgrn_pass(proj, 2, lb[1:2], s0b, n_seq, T, row0, True, o_prev=of, norm_g=hn)
            y_hgrn.append(ob)
            s_hgrn.append((sf, sb))
        ss.append(jnp.stack(s_hgrn[0], axis=1))

        merged = merge_branches(proj, blk("gc", GATE_RANK), y_att, y_ssd, y_hgrn, wb["gate"], wb["br_att"],
                                wb["br_ssd"], wb["br_hgrn"], l)
        x = matmul_residual(merged, wb["out"], l, x, g1, "out_proj")

        i = l // 2
        if l % 2 == 0:
            act = norm_glu_up(x, norm2_g[l], sc2, sh2, wb["ffn_g"], wb["ffn_u"], i)
            x = matmul_residual(act, wb["ffn_d"], i, x, g2, "ffn_down")
        else:
            x = moe_ffn(x, norm2_g[l], sc2, sh2, g2, router_w[i], wb["moe_g"], wb["moe_u"], wb["moe_d"], i)

    y_prompt = final_norm(x, final_g, 0, nc).reshape(BATCH, SEQ, D)
    y_sample = final_norm(x, final_g, nc, nl).reshape(DEC_BATCH, DEC_SEQ, D)
    return (y_prompt, y_sample, jnp.stack(ks, axis=1), jnp.stack(vs, axis=1),
            jnp.stack(hs, axis=1), jnp.stack(ss, axis=1))
```

```python
import functools

import numpy as np
import jax
import jax.numpy as jnp
from jax import lax
from jax.experimental import pallas as pl
from jax.experimental.pallas import tpu as pltpu

f32 = jnp.float32
bf16 = jnp.bfloat16

D_MODEL = 4096
BATCH = 16
SEQ = 256
DEPTH = 4
DEC_BATCH = 4
DEC_SEQ = 4096
PAST_LEN = 256
GRID_W = 64
N_HEADS = 16
N_KV = 4
HEAD_DIM = 64
WINDOW = 128
BLOCK = 128
ROPE_THETA = 10000.0
SSD_W = 1024
SSD_HEAD_DIM = 64
SSD_GROUPS = 2
SSD_STATE = 128
SSD_CONV = 5
SSD_CHUNK = 128
HGRN_W = 1024
HGRN_DK = 128
HGRN_DV = 128
HGRN_CHUNK = 16
GATE_RANK = 512
D_FF = 5632
N_EXPERTS = 8
TOP_K = 2
EXPERT_FF = 1408
EPS = 1e-6

LANE = 128
SUBLANE = 8
VMEM_LIMIT = 56 * 1024 * 1024
MOE_TILE = 512
HGRN_ROWS = 128
DT_PAD = 2 * LANE
HIGHEST = lax.Precision.HIGHEST


def _cparams(*sem):
    return pltpu.CompilerParams(dimension_semantics=sem, vmem_limit_bytes=VMEM_LIMIT)


def _pick(n, prefs):
    for p in prefs:
        if n % p == 0:
            return p
    return n


def _silu(x):
    return x * (1.0 / (1.0 + jnp.exp(-x)))


def _sigmoid(x):
    return 1.0 / (1.0 + jnp.exp(-x))


def _dims():
    att_w = N_HEADS * HEAD_DIM
    kv_w = N_KV * HEAD_DIM
    ssd_heads = SSD_W // SSD_HEAD_DIM
    conv_ch = SSD_W + 2 * SSD_GROUPS * SSD_STATE
    hgrn_heads = HGRN_W // HGRN_DK
    nc = BATCH * SEQ
    nl = DEC_BATCH * DEC_SEQ
    return att_w, kv_w, ssd_heads, conv_ch, hgrn_heads, nc, nl


def _group_of_tile(i, tm):
    nc = BATCH * SEQ
    r = i * tm
    return jnp.where(r < nc, 0, 1 + (r - nc) // DEC_SEQ)


def _row_tile(prefs=(512, 256, 128)):
    nc = BATCH * SEQ
    return _pick(int(np.gcd(nc, DEC_SEQ)), prefs)


def _ada_kernel(m_ref, w_ref, b_ref, o_ref):
    a = _silu(m_ref[...]).astype(bf16)
    o_ref[...] = jnp.dot(a, w_ref[...].astype(bf16), preferred_element_type=f32) + b_ref[...]


def ada_mod(m_pad, w_ada, b_ada):
    L, D, N = w_ada.shape
    G = m_pad.shape[0]
    tn = _pick(N, (512, 256, 128))
    return pl.pallas_call(
        _ada_kernel,
        grid=(L, N // tn),
        in_specs=[pl.BlockSpec((G, D), lambda l, j: (0, 0)),
                  pl.BlockSpec((None, D, tn), lambda l, j: (l, 0, j)),
                  pl.BlockSpec((None, 1, tn), lambda l, j: (l, 0, j))],
        out_specs=pl.BlockSpec((None, G, tn), lambda l, j: (l, 0, j)),
        out_shape=jax.ShapeDtypeStruct((L, G, N), f32),
        compiler_params=_cparams("parallel", "parallel"),
        name="ada_mod",
    )(m_pad, w_ada, b_ada.reshape(L, 1, N))


def _rms_mod(x, g, sc=None, sh=None):
    y = x * lax.rsqrt(jnp.mean(x * x, axis=-1, keepdims=True) + EPS) * g
    if sc is not None:
        y = y * (1.0 + sc) + sh
    return y


def _norm_kernel(x_ref, g_ref, o_ref):
    o_ref[...] = _rms_mod(x_ref[...], g_ref[...]).astype(o_ref.dtype)


def final_norm(x, g, row0, n_rows):
    D = x.shape[1]
    tm = _row_tile((256, 128))
    off = row0 // tm
    return pl.pallas_call(
        _norm_kernel,
        grid=(n_rows // tm,),
        in_specs=[pl.BlockSpec((tm, D), lambda i: (i + off, 0)), pl.BlockSpec((1, D), lambda i: (0, 0))],
        out_specs=pl.BlockSpec((tm, D), lambda i: (i, 0)),
        out_shape=jax.ShapeDtypeStruct((n_rows, D), f32),
        compiler_params=_cparams("parallel"),
        name="final_norm",
    )(x, g.reshape(1, D))


def _cast_kernel(x_ref, o_ref):
    o_ref[...] = x_ref[...].astype(o_ref.dtype)


def cast_bf16(w):
    shape = w.shape
    w2 = w.reshape(-1, shape[-1])
    R, C = w2.shape
    tr = _pick(R, (1024, 512, 256, 128, 64, 32, 16))
    tc = _pick(C, (2048, 1408, 1024, 512, 256, 128))
    out = pl.pallas_call(
        _cast_kernel,
        grid=(R // tr, C // tc),
        in_specs=[pl.BlockSpec((tr, tc), lambda i, j: (i, j))],
        out_specs=pl.BlockSpec((tr, tc), lambda i, j: (i, j)),
        out_shape=jax.ShapeDtypeStruct((R, C), bf16),
        compiler_params=_cparams("parallel", "parallel"),
        name="cast_bf16",
    )(w2)
    return out.reshape(shape)


def _norm_router_kernel(x_ref, g_ref, sc_ref, sh_ref, wr_ref, h_ref, lg_ref):
    h = _rms_mod(x_ref[...], g_ref[...], sc_ref[...], sh_ref[...])
    h_ref[...] = h
    lg_ref[...] = jnp.dot(h, wr_ref[...], preferred_element_type=f32, precision=HIGHEST)


def norm_router(x, g, sc, sh, w_router_pad):
    M, D = x.shape
    tm = _row_tile((256, 128))
    row = pl.BlockSpec((tm, D), lambda i: (i, 0))
    grp = pl.BlockSpec((None, 1, D), lambda i: (_group_of_tile(i, tm), 0, 0))
    return pl.pallas_call(
        _norm_router_kernel,
        grid=(M // tm,),
        in_specs=[row, pl.BlockSpec((1, D), lambda i: (0, 0)), grp, grp,
                  pl.BlockSpec((D, LANE), lambda i: (0, 0))],
        out_specs=[row, pl.BlockSpec((tm, LANE), lambda i: (i, 0))],
        out_shape=[jax.ShapeDtypeStruct((M, D), f32), jax.ShapeDtypeStruct((M, LANE), f32)],
        compiler_params=_cparams("parallel"),
        name="norm_router",
    )(x, g.reshape(1, D), sc, sh, w_router_pad)


class _NormPipe:
    def __init__(self, M, D, nj):
        self.tm = _row_tile((1024, 512, 256, 128))
        self.n_tiles = M // self.tm
        self.chunks = max(c for c in (8, 4, 2, 1) if c <= nj and self.tm % (c * SUBLANE) == 0)
        self.rows = self.tm // self.chunks
        self.D = D
        self.nj = nj

    def tile(self, it):
        return jnp.minimum(it, self.n_tiles - 1)

    def in_specs(self):
        D, tm = self.D, self.tm
        grp = pl.BlockSpec((None, 1, D), lambda it, j: (_group_of_tile(self.tile(it), tm), 0, 0))
        x_chunk = pl.BlockSpec(
            (self.rows, D), lambda it, j: (self.tile(it) * self.chunks + jnp.minimum(j, self.chunks - 1), 0))
        return [x_chunk, pl.BlockSpec((1, D), lambda it, j: (0, 0)), grp, grp]

    def out_spec(self, tn):
        return pl.BlockSpec((self.tm, tn), lambda it, j: (jnp.where(it == 0, self.n_tiles, it - 1), j))

    def out_rows(self):
        return (self.n_tiles + 1) * self.tm

    def scratch(self):
        return [pltpu.VMEM((self.tm, self.D), bf16), pltpu.VMEM((self.tm, self.D), bf16)]

    def grid(self):
        return (self.n_tiles + 1, self.nj)


def _norm_pipelined(chunks, product, x_ref, g_ref, sc_ref, sh_ref, o_ref, h_even, h_odd):
    it, j = pl.program_id(0), pl.program_id(1)
    rows = x_ref.shape[0]
    r0 = pl.multiple_of(jnp.minimum(j, chunks - 1) * rows, rows)

    def step(fill, ready):
        fill[pl.ds(r0, rows), :] = _rms_mod(x_ref[...], g_ref[...], sc_ref[...], sh_ref[...]).astype(bf16)
        if ready is None:
            o_ref[...] = jnp.zeros(o_ref.shape, o_ref.dtype)
        else:
            o_ref[...] = product(ready[...]).astype(o_ref.dtype)

    @pl.when(it == 0)
    def _():
        step(h_even, None)

    @pl.when((it > 0) & (it % 2 == 0))
    def _():
        step(h_even, h_odd)

    @pl.when(it % 2 == 1)
    def _():
        step(h_odd, h_even)


def _norm_mm_kernel(chunks, x_ref, g_ref, sc_ref, sh_ref, w_ref, o_ref, h_even, h_odd):
    def product(h):
        return jnp.dot(h, w_ref[...], preferred_element_type=f32)

    _norm_pipelined(chunks, product, x_ref, g_ref, sc_ref, sh_ref, o_ref, h_even, h_odd)


def norm_matmul(x, g, sc, sh, w):
    M, D = x.shape
    N = w.shape[1]
    tn = _pick(N, (768, 1024, 512, 256, 128))
    plan = _NormPipe(M, D, N // tn)
    return pl.pallas_call(
        functools.partial(_norm_mm_kernel, plan.chunks),
        grid=plan.grid(),
        in_specs=plan.in_specs() + [pl.BlockSpec((D, tn), lambda it, j: (0, j))],
        out_specs=plan.out_spec(tn),
        out_shape=jax.ShapeDtypeStruct((plan.out_rows(), N), f32),
        scratch_shapes=plan.scratch(),
        compiler_params=_cparams("arbitrary", "arbitrary"),
        name="in_proj",
    )(x, g.reshape(1, D), sc, sh, w)


def _norm_glu_kernel(chunks, x_ref, g_ref, sc_ref, sh_ref, wg_ref, wu_ref, o_ref, h_even, h_odd):
    def product(h):
        a = jnp.dot(h, wg_ref[...], preferred_element_type=f32)
        u = jnp.dot(h, wu_ref[...], preferred_element_type=f32)
        return _silu(a) * u

    _norm_pipelined(chunks, product, x_ref, g_ref, sc_ref, sh_ref, o_ref, h_even, h_odd)


def norm_glu_up(x, g, sc, sh, wg, wu, layer):
    M, D = x.shape
    N = wg.shape[2]
    tn = _pick(N, (512, 256, 128))
    plan = _NormPipe(M, D, N // tn)
    wspec = pl.BlockSpec((None, D, tn), lambda it, j: (layer, 0, j))
    return pl.pallas_call(
        functools.partial(_norm_glu_kernel, plan.chunks),
        grid=plan.grid(),
        in_specs=plan.in_specs() + [wspec, wspec],
        out_specs=plan.out_spec(tn),
        out_shape=jax.ShapeDtypeStruct((plan.out_rows(), N), bf16),
        scratch_shapes=plan.scratch(),
        compiler_params=_cparams("arbitrary", "arbitrary"),
        name="glu_up",
    )(x, g.reshape(1, D), sc, sh, wg, wu)


def _mm_res_kernel(a_ref, w_ref, x_ref, g_ref, o_ref):
    y = jnp.dot(a_ref[...], w_ref[...], preferred_element_type=f32)
    o_ref[...] = x_ref[...] + g_ref[...] * y


def matmul_residual(a, w, layer, x, gate, name="matmul_residual"):
    M, K = x.shape[0], a.shape[1]
    N = w.shape[2]
    tm = _row_tile((512, 256, 128))
    tn = _pick(N, (1024, 512, 256, 128))
    return pl.pallas_call(
        _mm_res_kernel,
        grid=(N // tn, M // tm),
        in_specs=[pl.BlockSpec((tm, K), lambda j, i: (i, 0)),
                  pl.BlockSpec((None, K, tn), lambda j, i: (layer, 0, j)),
                  pl.BlockSpec((tm, tn), lambda j, i: (i, j)),
                  pl.BlockSpec((None, 1, tn), lambda j, i: (_group_of_tile(i, tm), 0, j))],
        out_specs=pl.BlockSpec((tm, tn), lambda j, i: (i, j)),
        out_shape=jax.ShapeDtypeStruct((M, N), f32),
        compiler_params=_cparams("parallel", "parallel"),
        name=name,
    )(a, w, x, gate)


def _merge_kernel(n_ctx_tiles, gc_ref, yac_ref, yal_ref, ysc_ref, ysl_ref, yhc_ref, yhl_ref,
                  wga_ref, wgs_ref, wgh_ref, wa_ref, ws_ref, wh_ref, o_ref):
    gc = gc_ref[...].astype(bf16)
    is_ctx = pl.program_id(1) < n_ctx_tiles

    def branch(yc_ref, yl_ref, wg_ref, w_ref):
        y = jnp.where(is_ctx, yc_ref[...], yl_ref[...])
        gate = _sigmoid(jnp.dot(gc, wg_ref[...], preferred_element_type=f32))
        return gate * jnp.dot(y, w_ref[...], preferred_element_type=f32)

    o_ref[...] = (branch(yac_ref, yal_ref, wga_ref, wa_ref) + branch(ysc_ref, ysl_ref, wgs_ref, ws_ref)
                  + branch(yhc_ref, yhl_ref, wgh_ref, wh_ref)).astype(o_ref.dtype)


def merge_branches(proj, gc_col, y_att, y_ssd, y_hgrn, w_gate, w_att, w_ssd, w_hgrn, layer):
    M = y_att[0].shape[0] + y_att[1].shape[0]
    R = w_gate.shape[1]
    D = w_att.shape[2]
    tm = _row_tile((512, 256, 128))
    tn = _pick(D, (1024, 512, 256, 128))
    nj = D // tn
    n_ctx = (BATCH * SEQ) // tm
    n_lat = (DEC_BATCH * DEC_SEQ) // tm

    def pair(y):
        w = y[0].shape[1]
        return [pl.BlockSpec((tm, w), lambda j, i: (jnp.minimum(i, n_ctx - 1), 0)),
                pl.BlockSpec((tm, w), lambda j, i: (jnp.clip(i - n_ctx, 0, n_lat - 1), 0))]

    def gate_cols(b):
        return pl.BlockSpec((None, R, tn), lambda j, i: (layer, 0, j + b * nj))

    def cols(w):
        return pl.BlockSpec((None, w.shape[1], tn), lambda j, i: (layer, 0, j))

    return pl.pallas_call(
        functools.partial(_merge_kernel, n_ctx),
        grid=(nj, M // tm),
        in_specs=[pl.BlockSpec((tm, R), lambda j, i: (i, gc_col))] + pair(y_att) + pair(y_ssd) + pair(y_hgrn)
                 + [gate_cols(0), gate_cols(1), gate_cols(2), cols(w_att), cols(w_ssd), cols(w_hgrn)],
        out_specs=pl.BlockSpec((tm, tn), lambda j, i: (i, j)),
        out_shape=jax.ShapeDtypeStruct((M, D), bf16),
        compiler_params=_cparams("parallel", "parallel"),
        name="merge_branches",
    )(proj, *y_att, *y_ssd, *y_hgrn, w_gate, w_gate, w_gate, w_att, w_ssd, w_hgrn)


def _softmax_parts(parts, sink):
    m = sink
    for s in parts:
        m = jnp.maximum(m, jnp.max(s, axis=-1, keepdims=True))
    ps = [jnp.exp(s - m) for s in parts]
    den = jnp.exp(sink - m)
    for p in ps:
        den = den + jnp.sum(p, axis=-1, keepdims=True)
    return ps, 1.0 / den


def _attend_group(sink_ref, kv, q_heads, pieces, o_ref):
    g = len(q_heads)
    outs = []
    for gi, q in enumerate(q_heads):
        scores = []
        for k, _, bias in pieces:
            s = lax.dot_general(q, k, (((1,), (1,)), ((), ())), preferred_element_type=f32)
            scores.append(s if bias is None else s + bias)
        ps, inv = _softmax_parts(scores, sink_ref[kv * g + gi])
        o = jnp.dot(ps[0].astype(bf16), pieces[0][1], preferred_element_type=f32)
        for p, (_, v, _) in zip(ps[1:], pieces[1:]):
            o = o + jnp.dot(p.astype(bf16), v, preferred_element_type=f32)
        outs.append(o * inv)
    per_lane = LANE // HEAD_DIM
    for pi in range(g // per_lane):
        tile = jnp.concatenate(outs[pi * per_lane:(pi + 1) * per_lane], axis=1)
        c0 = (kv * g + pi * per_lane) * HEAD_DIM
        o_ref[:, c0:c0 + LANE] = tile.astype(o_ref.dtype)


def _attn_ctx_kernel(sink_ref, q_ref, k_ref, v_ref, o_ref):
    g = N_HEADS // N_KV
    scale = HEAD_DIM ** -0.5
    for kv in range(N_KV):
        sl = slice(kv * HEAD_DIM, (kv + 1) * HEAD_DIM)
        k = k_ref[:, sl].astype(bf16)
        v = v_ref[:, sl].astype(bf16)
        qs = [(q_ref[:, (kv * g + gi) * HEAD_DIM:(kv * g + gi + 1) * HEAD_DIM] * scale).astype(bf16)
              for gi in range(g)]
        _attend_group(sink_ref, kv, qs, [(k, v, None)], o_ref)


def attention_context(proj, q_col, k_col, sink):
    att_w, kv_w = N_HEADS * HEAD_DIM, N_KV * HEAD_DIM
    return pl.pallas_call(
        _attn_ctx_kernel,
        grid=(BATCH,),
        in_specs=[pl.BlockSpec(memory_space=pltpu.SMEM),
                  pl.BlockSpec((SEQ, att_w), lambda b: (b, q_col)),
                  pl.BlockSpec((SEQ, kv_w), lambda b: (b, k_col)),
                  pl.BlockSpec((SEQ, kv_w), lambda b: (b, k_col + 1))],
        out_specs=pl.BlockSpec((SEQ, att_w), lambda b: (b, 0)),
        out_shape=jax.ShapeDtypeStruct((BATCH * SEQ, att_w), bf16),
        compiler_params=_cparams("parallel"),
        name="attention_context",
    )(sink, proj, proj, proj)


def _rope_kernel(n_q, n_k, q_ref, kv_ref, cos_ref, sin_ref, qo_ref, ko_ref, vt_ref):
    cos = cos_ref[...]
    sin = sin_ref[...]
    quarter = HEAD_DIM // 4
    lane = lax.broadcasted_iota(jnp.int32, cos.shape, 1)
    first = (lane % (2 * quarter)) < quarter

    def rot(x):
        swapped = jnp.where(first, pltpu.roll(x, LANE - quarter, axis=1), pltpu.roll(x, quarter, axis=1))
        return x * cos + swapped * sin

    scale = HEAD_DIM ** -0.5
    for j in range(n_q):
        sl = slice(j * LANE, (j + 1) * LANE)
        qo_ref[:, sl] = (rot(q_ref[:, sl]) * scale).astype(qo_ref.dtype)
    for j in range(n_k):
        sl = slice(j * LANE, (j + 1) * LANE)
        ko_ref[:, sl] = rot(kv_ref[:, sl]).astype(ko_ref.dtype)
    vt_ref[...] = kv_ref[:, n_k * LANE:].T.astype(vt_ref.dtype)


def rope_latent(proj, q_col, kv_col, cos, sin, row0):
    att_w, kv_w = N_HEADS * HEAD_DIM, N_KV * HEAD_DIM
    nl = DEC_BATCH * DEC_SEQ
    tm = _pick(DEC_SEQ, (512, 256, 128))
    off = row0 // tm
    per_seq = DEC_SEQ // tm
    tab = pl.BlockSpec((tm, LANE), lambda i: (i % per_seq, 0))
    return pl.pallas_call(
        functools.partial(_rope_kernel, att_w // LANE, kv_w // LANE),
        grid=(nl // tm,),
        in_specs=[pl.BlockSpec((tm, att_w), lambda i: (i + off, q_col)),
                  pl.BlockSpec((tm, 2 * kv_w), lambda i: (i + off, kv_col)), tab, tab],
        out_specs=[pl.BlockSpec((tm, att_w), lambda i: (i, 0)), pl.BlockSpec((tm, kv_w), lambda i: (i, 0)),
                   pl.BlockSpec((kv_w, tm), lambda i: (0, i))],
        out_shape=[jax.ShapeDtypeStruct((nl, att_w), bf16), jax.ShapeDtypeStruct((nl, kv_w), bf16),
                   jax.ShapeDtypeStruct((kv_w, nl), bf16)],
        compiler_params=_cparams("parallel"),
        name="rope_latent",
    )(proj, proj, cos, sin)


def _attn_lat_kernel(nb, sink_ref, q_ref, kp_ref, kc_ref, kn_ref, vtp_ref, vtc_ref, vtn_ref, kctx_ref, vtctx_ref,
                     o_ref):
    n = pl.program_id(1)
    g = N_HEADS // N_KV
    kj = lax.broadcasted_iota(jnp.int32, (3 * BLOCK, BLOCK), 0)
    qi = lax.broadcasted_iota(jnp.int32, (3 * BLOCK, BLOCK), 1)
    valid = (jnp.abs(kj - BLOCK - qi) <= WINDOW)
    valid = valid & ((kj >= BLOCK) | (n > 0)) & ((kj < 2 * BLOCK) | (n < nb - 1))
    bias = jnp.where(valid, 0.0, -1e30).astype(f32)
    k_loc_all = jnp.concatenate([kp_ref[...], kc_ref[...], kn_ref[...]], axis=0)
    vt_loc_all = jnp.concatenate([vtp_ref[...], vtc_ref[...], vtn_ref[...]], axis=1)
    nt = (((1,), (1,)), ((), ()))
    per_lane = LANE // HEAD_DIM
    for kv in range(N_KV):
        sl = slice(kv * HEAD_DIM, (kv + 1) * HEAD_DIM)
        k_loc, k_ctx = k_loc_all[:, sl], kctx_ref[:, sl]
        vt_loc, vt_ctx = vt_loc_all[sl, :], vtctx_ref[sl, :]
        scores, probs, outs = [], [], []
        for gi in range(g):
            h = kv * g + gi
            q = q_ref[:, h * HEAD_DIM:(h + 1) * HEAD_DIM]
            s_ctx = lax.dot_general(k_ctx, q, nt, preferred_element_type=f32)
            s_loc = lax.dot_general(k_loc, q, nt, preferred_element_type=f32) + bias
            scores.append((s_ctx, s_loc))
        for gi in range(g):
            s_ctx, s_loc = scores[gi]
            sink = sink_ref[kv * g + gi]
            m = jnp.maximum(jnp.maximum(jnp.max(s_ctx, axis=0, keepdims=True),
                                        jnp.max(s_loc, axis=0, keepdims=True)), sink)
            p_ctx = jnp.exp(s_ctx - m)
            p_loc = jnp.exp(s_loc - m)
            den = (jnp.sum(p_ctx, axis=0, keepdims=True) + jnp.sum(p_loc, axis=0, keepdims=True)
                   + jnp.exp(sink - m))
            probs.append((p_ctx.astype(bf16), p_loc.astype(bf16), 1.0 / den))
        for gi in range(g):
            p_ctx, p_loc, inv = probs[gi]
            o_t = (jnp.dot(vt_ctx, p_ctx, preferred_element_type=f32)
                   + jnp.dot(vt_loc, p_loc, preferred_element_type=f32))
            outs.append(o_t * inv)
        for pi in range(g // per_lane):
            tile = jnp.concatenate(outs[pi * per_lane:(pi + 1) * per_lane], axis=0).T
            c0 = (kv * g + pi * per_lane) * HEAD_DIM
            o_ref[:, c0:c0 + LANE] = tile.astype(o_ref.dtype)


def attention_latent(q_rot, k_rot, vt, k_ctx, vt_ctx, sink):
    att_w, kv_w = q_rot.shape[1], k_rot.shape[1]
    nb = DEC_SEQ // BLOCK
    past = k_ctx.shape[1]
    prev = lambda n: jnp.maximum(n - 1, 0)
    same = lambda n: n
    nxt = lambda n: jnp.minimum(n + 1, nb - 1)
    kspec = lambda f: pl.BlockSpec((BLOCK, kv_w), lambda b, n: (b * nb + f(n), 0))
    vspec = lambda f: pl.BlockSpec((kv_w, BLOCK), lambda b, n: (0, b * nb + f(n)))
    return pl.pallas_call(
        functools.partial(_attn_lat_kernel, nb),
        grid=(DEC_BATCH, nb),
        in_specs=[pl.BlockSpec(memory_space=pltpu.SMEM),
                  pl.BlockSpec((BLOCK, att_w), lambda b, n: (b * nb + n, 0)),
                  kspec(prev), kspec(same), kspec(nxt), vspec(prev), vspec(same), vspec(nxt),
                  pl.BlockSpec((None, past, kv_w), lambda b, n: (b, 0, 0)),
                  pl.BlockSpec((None, kv_w, past), lambda b, n: (b, 0, 0))],
        out_specs=pl.BlockSpec((BLOCK, att_w), lambda b, n: (b * nb + n, 0)),
        out_shape=jax.ShapeDtypeStruct((DEC_BATCH * DEC_SEQ, att_w), bf16),
        compiler_params=_cparams("parallel", "parallel"),
        name="attention_latent",
    )(sink, q_rot, k_rot, k_rot, k_rot, vt, vt, vt, k_ctx, vt_ctx)


def _conv_silu(p_ref, c_ref, n_ref, w_ref, b_ref, has_prev, has_next):
    L = SSD_CHUNK
    lo = SUBLANE - SSD_CONV // 2
    xp = jnp.where(has_prev, p_ref[...], 0.0)
    xn = jnp.where(has_next, n_ref[...], 0.0)
    ext = jnp.concatenate([xp, c_ref[...], xn], axis=0)
    acc = b_ref[...] + ext[lo: lo + L] * w_ref[0:1, :]
    for k in range(1, SSD_CONV):
        acc = acc + ext[lo + k: lo + k + L] * w_ref[k:k + 1, :]
    return _silu(acc)


def _ssd_kernel(reverse, final, nc, *refs):
    (xp_ref, xc_ref, xn_ref, bp_ref, bc_ref, bn_ref, dt_ref, cwx_ref, cbx_ref, cwb_ref, cbb_ref,
     dtb_ref, a_ref, ex_ref, h0_ref) = refs[:15]
    if final:
        dskip_ref, z_ref, yprev_ref, norm_ref, y_ref, hfin_ref, st_ref, ybuf_ref = refs[15:]
    else:
        y_ref, hfin_ref, st_ref = refs[15:]
        ybuf_ref = y_ref
    L = SSD_CHUNK
    heads = SSD_W // SSD_HEAD_DIM
    per_lane = LANE // SSD_HEAD_DIM
    n_pairs = heads // per_lane
    pairs_per_group = n_pairs // SSD_GROUPS
    c = pl.program_id(1)
    cc = (nc - 1 - c) if reverse else c

    @pl.when(c == 0)
    def _():
        st_ref[...] = h0_ref[...]

    xs = _conv_silu(xp_ref, xc_ref, xn_ref, cwx_ref, cbx_ref, cc > 0, cc < nc - 1)
    bcm = _conv_silu(bp_ref, bc_ref, bn_ref, cwb_ref, cbb_ref, cc > 0, cc < nc - 1)

    raw = dt_ref[...] + dtb_ref[...]
    dt = jnp.maximum(raw, 0.0) + jnp.log(1.0 + jnp.exp(-jnp.abs(raw)))
    ri = lax.broadcasted_iota(jnp.int32, (L, L), 0)
    ci = lax.broadcasted_iota(jnp.int32, (L, L), 1)
    causal = (ci >= ri) if reverse else (ci <= ri)
    tri = jnp.where(causal, 1.0, 0.0).astype(f32)
    acum = jnp.dot(tri, dt * a_ref[...], preferred_element_type=f32, precision=HIGHEST)
    acum_t = acum.T
    wide = jnp.dot(jnp.concatenate([acum, dt], axis=0), ex_ref[...], preferred_element_type=f32,
                   precision=HIGHEST)
    acum_x, dt_x = wide[:L], wide[L:]
    last = 0 if reverse else L - 1
    lane0 = heads if reverse else 0
    tot_x = acum_x[last:last + 1, :]
    xdt = xs * dt_x
    xdt_end = (xdt * jnp.exp(tot_x - acum_x)).astype(bf16)
    in_scale = jnp.exp(acum_x)
    st_scale = jnp.exp(tot_x)
    low_half = lax.broadcasted_iota(jnp.int32, (L, LANE), 1) < SSD_HEAD_DIM

    for g in range(SSD_GROUPS):
        bm_f = bcm[:, g * SSD_STATE:(g + 1) * SSD_STATE]
        bm = bm_f.astype(bf16)
        bm_t = bm_f.T.astype(bf16)
        cm_off = SSD_GROUPS * SSD_STATE
        cm = bcm[:, cm_off + g * SSD_STATE: cm_off + (g + 1) * SSD_STATE].astype(bf16)
        cb = lax.dot_general(cm, bm, (((1,), (1,)), ((), ())), preferred_element_type=f32)
        cb = jnp.where(causal, cb, 0.0)
        for pp in range(pairs_per_group):
            pr = g * pairs_per_group + pp
            cols = slice(pr * LANE, (pr + 1) * LANE)
            ws = []
            for hh in range(per_lane):
                ln = lane0 + pr * per_lane + hh
                seg = acum[:, ln:ln + 1] - acum_t[ln:ln + 1, :]
                ws.append((cb * jnp.exp(jnp.minimum(seg, 0.0))).astype(bf16))
            x2 = xdt[:, cols]
            xbd = jnp.concatenate([jnp.where(low_half, x2, 0.0).astype(bf16),
                                   jnp.where(low_half, 0.0, x2).astype(bf16)], axis=0)
            y = jnp.dot(jnp.concatenate(ws, axis=1), xbd, preferred_element_type=f32)
            st = st_ref[pr]
            y = y + jnp.dot(cm, st.astype(bf16), preferred_element_type=f32) * in_scale[:, cols]
            st_ref[pr] = st * st_scale[:, cols] + jnp.dot(bm_t, xdt_end[:, cols], preferred_element_type=f32)
            ybuf_ref[:, cols] = y

    @pl.when(c == nc - 1)
    def _():
        hfin_ref[...] = st_ref[...]

    if final:
        yt = (ybuf_ref[...] + yprev_ref[...] + dskip_ref[...] * xs) * _silu(z_ref[...])
        yn = yt * lax.rsqrt(jnp.mean(yt * yt, axis=-1, keepdims=True) + EPS) * norm_ref[...]
        y_ref[...] = yn.astype(y_ref.dtype)


def ssd_state_to_lanes(h):
    n, heads, P, N = h.shape
    per_lane = LANE // P
    return h.reshape(n, heads // per_lane, per_lane, P, N).transpose(0, 1, 4, 2, 3).reshape(
        n, heads // per_lane, N, per_lane * P)


def ssd_state_from_lanes(h):
    n, pairs, N, w = h.shape
    per_lane = w // SSD_HEAD_DIM
    return h.reshape(n, pairs, N, per_lane, SSD_HEAD_DIM).transpose(0, 1, 3, 4, 2).reshape(
        n, pairs * per_lane, SSD_HEAD_DIM, N)


def ssd_pass(proj, cols, prm, h0, n_seq, T, row0, reverse, y_prev=None):
    final = y_prev is not None
    L = SSD_CHUNK
    nc = T // L
    bc_w = 2 * SSD_GROUPS * SSD_STATE
    heads = SSD_W // SSD_HEAD_DIM
    assert LANE // SSD_HEAD_DIM == 2 and (heads // 2) % SSD_GROUPS == 0
    hb = L // SUBLANE
    c0 = row0 // L
    last_blk = (proj.shape[0] // SUBLANE) - 1

    def chunk(c):
        return (nc - 1 - c) if reverse else c

    def cur(w, col):
        return pl.BlockSpec((L, w), lambda s, c: (c0 + s * nc + chunk(c), col))

    def prev(w, col):
        return pl.BlockSpec((SUBLANE, w), lambda s, c: (jnp.maximum((c0 + s * nc + chunk(c)) * hb - 1, 0), col))

    def nxt(w, col):
        return pl.BlockSpec((SUBLANE, w),
                            lambda s, c: (jnp.minimum((c0 + s * nc + chunk(c) + 1) * hb, last_blk), col))

    out_rows = pl.BlockSpec((L, SSD_W), lambda s, c: (s * nc + chunk(c), 0))
    full = lambda a: pl.BlockSpec(a.shape, lambda s, c: (0,) * a.ndim)
    st_spec = pl.BlockSpec((None,) + h0.shape[1:], lambda s, c: (s, 0, 0, 0))
    d = 1 if reverse else 0
    consts = [prm["cw_x"], prm["cb_x"], prm["cw_bc"], prm["cb_bc"], prm["dt_bias"], prm["a_lane"], prm["expand"][d]]
    in_specs = ([prev(SSD_W, cols["xs"]), cur(SSD_W, cols["xs"]), nxt(SSD_W, cols["xs"]),
                 prev(bc_w, cols["bc"]), cur(bc_w, cols["bc"]), nxt(bc_w, cols["bc"]), cur(LANE, cols["dt"])]
                + [full(a) for a in consts] + [st_spec])
    args = [proj] * 7 + consts + [h0]
    scratch = [pltpu.VMEM(h0.shape[1:], f32)]
    if final:
        in_specs += [full(prm["dskip"]), cur(SSD_W, cols["z"]), out_rows, full(prm["norm"])]
        args += [prm["dskip"], proj, y_prev, prm["norm"]]
        scratch.append(pltpu.VMEM((L, SSD_W), f32))
    return pl.pallas_call(
        functools.partial(_ssd_kernel, reverse, final, nc),
        grid=(n_seq, nc),
        in_specs=in_specs,
        out_specs=[out_rows, st_spec],
        out_shape=[jax.ShapeDtypeStruct((n_seq * T, SSD_W), bf16 if final else f32),
                   jax.ShapeDtypeStruct(h0.shape, f32)],
        scratch_shapes=scratch,
        compiler_params=_cparams("parallel", "arbitrary"),
        name="ssd_final" if final else "ssd_first",
    )(*args)


def _hgrn_kernel(reverse, final, nblk, *refs):
    hq_ref, hf_ref, hi_ref, lb_ref, s0_ref = refs[:5]
    if final:
        hg_ref, oprev_ref, norm_ref, y_ref, sfin_ref, st_ref, q_s, k_s, bc_s, f_s, o_s = refs[5:]
    else:
        y_ref, sfin_ref, st_ref, q_s, k_s, bc_s, f_s = refs[5:]
        o_s = y_ref
    R = HGRN_ROWS
    C = HGRN_CHUNK
    nsub = R // C
    heads = HGRN_W // HGRN_DK
    DK = HGRN_DK
    b = pl.program_id(1)

    @pl.when(b == 0)
    def _():
        for h in range(heads):
            st_ref[h] = s0_ref[h].T

    r = hf_ref[...]
    lb = lb_ref[...]
    e = jnp.exp(-jnp.abs(r))
    inv = 1.0 / (1.0 + e)
    sig_pos = jnp.where(r >= 0, inv, e * inv)
    sig_neg = jnp.where(r >= 0, e * inv, inv)
    f = lb + (1.0 - lb) * sig_pos
    f_s[...] = f
    logf = jnp.log(f)
    k_s[...] = (1.0 - lb) * sig_neg
    q_s[...] = _silu(hq_ref[...])
    ri = lax.broadcasted_iota(jnp.int32, (R, R), 0)
    ci = lax.broadcasted_iota(jnp.int32, (R, R), 1)
    same = (ri // C) == (ci // C)
    tri = jnp.where(same & ((ci >= ri) if reverse else (ci <= ri)), 1.0, 0.0).astype(f32)
    bc_s[...] = jnp.dot(tri, logf, preferred_element_type=f32, precision=HIGHEST)

    ii = lax.broadcasted_iota(jnp.int32, (SUBLANE, DK), 0)
    last = 0 if reverse else C - 1

    for t in range(nsub):
        r0 = ((nsub - 1 - t) if reverse else t) * C
        rows = pl.ds(r0, C)
        for h in range(heads):
            hs = slice(h * DK, (h + 1) * DK)
            q = q_s[rows, hs]
            k = k_s[rows, hs]
            bc = bc_s[rows, hs]
            v = hi_ref[rows, hs]
            st = st_ref[h]
            o = lax.dot_general((q * jnp.exp(bc)).astype(bf16), st.astype(bf16), (((1,), (1,)), ((), ())),
                                preferred_element_type=f32)
            groups = [slice(gi * SUBLANE, (gi + 1) * SUBLANE) for gi in range(C // SUBLANE)]
            og = [o[s] for s in groups]
            qd = [None] * len(groups)
            for j in (range(C) if reverse else range(C - 1, -1, -1)):
                gj, jj = divmod(j, SUBLANE)
                step = j - 1 if reverse else j + 1
                k_row = k_s[pl.ds(r0 + j, 1), hs]
                v_row = hi_ref[pl.ds(r0 + j, 1), hs]
                f_row = f_s[pl.ds(r0 + step, 1), hs] if 0 <= step < C else None
                for gi in (range(gj + 1) if reverse else range(gj, len(groups))):
                    prev = None if qd[gi] is None else qd[gi] * f_row
                    if gi == gj:
                        qd[gi] = jnp.where(ii == jj, q[groups[gi]], 0.0 if prev is None else prev)
                    else:
                        qd[gi] = prev
                    a = jnp.sum(qd[gi] * k_row, axis=-1, keepdims=True)
                    og[gi] = og[gi] + a * v_row
            o_s[rows, hs] = jnp.concatenate(og, axis=0)
            bl = bc[last:last + 1, :]
            kd = (k * jnp.exp(bl - bc)).astype(bf16)
            upd = lax.dot_general(v.astype(bf16), kd, (((0,), (0,)), ((), ())), preferred_element_type=f32)
            st_ref[h] = st * jnp.exp(bl) + upd

    @pl.when(b == nblk - 1)
    def _():
        for h in range(heads):
            sfin_ref[h] = st_ref[h].T

    if final:
        ot = o_s[...] + oprev_ref[...]
        on = ot * lax.rsqrt(jnp.mean(ot * ot, axis=-1, keepdims=True) + EPS) * norm_ref[...]
        y_ref[...] = (on * _silu(hg_ref[...])).astype(y_ref.dtype)


def hgrn_pass(ph, f_col, lb, s0, n_seq, T, row0, reverse, o_prev=None, norm_g=None):
    final = o_prev is not None
    R = HGRN_ROWS
    nblk = T // R
    heads = HGRN_W // HGRN_DK
    b0 = row0 // R

    def blk(b):
        return (nblk - 1 - b) if reverse else b

    def col(j):
        return pl.BlockSpec((R, HGRN_W), lambda s, b: (b0 + s * nblk + blk(b), j))

    out_rows = pl.BlockSpec((R, HGRN_W), lambda s, b: (s * nblk + blk(b), 0))
    full = lambda a: pl.BlockSpec(a.shape, lambda s, b: (0,) * a.ndim)
    st_spec = pl.BlockSpec((None, heads, HGRN_DK, HGRN_DV), lambda s, b: (s, 0, 0, 0))
    in_specs = [col(0), col(f_col), col(3), full(lb), st_spec]
    args = [ph, ph, ph, lb, s0]
    scratch = [pltpu.VMEM((heads, HGRN_DV, HGRN_DK), f32)] + [pltpu.VMEM((R, HGRN_W), f32)] * 4
    if final:
        in_specs += [col(4), out_rows, full(norm_g)]
        args += [ph, o_prev, norm_g]
        scratch.append(pltpu.VMEM((R, HGRN_W), f32))
    return pl.pallas_call(
        functools.partial(_hgrn_kernel, reverse, final, nblk),
        grid=(n_seq, nblk),
        in_specs=in_specs,
        out_specs=[out_rows, st_spec],
        out_shape=[jax.ShapeDtypeStruct((n_seq * T, HGRN_W), bf16 if final else f32),
                   jax.ShapeDtypeStruct((n_seq, heads, HGRN_DK, HGRN_DV), f32)],
        scratch_shapes=scratch,
        compiler_params=_cparams("parallel", "arbitrary"),
        name="hgrn_final" if final else "hgrn_first",
    )(*args)


def _row_copy(src_hbm, dst, sem, src_row, dst_row):
    return pltpu.make_async_copy(src_hbm.at[pl.ds(src_row, 1), :], dst.at[pl.ds(dst_row, 1), :], sem)


def _moe_gather_kernel(nv_ref, idx_ref, idx_next_ref, h_hbm, o_ref, buf, sem):
    n = buf.shape[1]
    t = pl.program_id(0)
    n_valid = nv_ref[0]
    slot = t % 2

    def issue(idx, s):
        def start(r, c):
            _row_copy(h_hbm, buf.at[s], sem.at[s], idx[0, 0, r], r).start()
            return c

        lax.fori_loop(0, n, start, 0)

    @pl.when((t == 0) & (n_valid > 0))
    def _():
        issue(idx_ref, 0)

    @pl.when(t + 1 < n_valid)
    def _():
        issue(idx_next_ref, 1 - slot)

    @pl.when(t < n_valid)
    def _():
        def wait(r, c):
            _row_copy(h_hbm, buf.at[slot], sem.at[slot], 0, r).wait()
            return c

        lax.fori_loop(0, n, wait, 0)
        o_ref[...] = buf[slot].astype(o_ref.dtype)

    @pl.when(t >= n_valid)
    def _():
        o_ref[...] = jnp.zeros(o_ref.shape, o_ref.dtype)


def moe_gather(n_valid, h2, slot_tok):
    D = h2.shape[1]
    n_tiles, _, tm = slot_tok.shape
    return pl.pallas_call(
        _moe_gather_kernel,
        grid_spec=pltpu.PrefetchScalarGridSpec(
            num_scalar_prefetch=1, grid=(n_tiles,),
            in_specs=[pl.BlockSpec((1, 1, tm), lambda t, nv: (t, 0, 0), memory_space=pltpu.SMEM),
                      pl.BlockSpec((1, 1, tm), lambda t, nv: (jnp.minimum(t + 1, n_tiles - 1), 0, 0),
                                   memory_space=pltpu.SMEM),
                      pl.BlockSpec(memory_space=pl.ANY)],
            out_specs=pl.BlockSpec((tm, D), lambda t, nv: (t, 0)),
            scratch_shapes=[pltpu.VMEM((2, tm, D), h2.dtype), pltpu.SemaphoreType.DMA((2,))]),
        out_shape=jax.ShapeDtypeStruct((n_tiles * tm, D), bf16),
        compiler_params=_cparams("arbitrary"),
        name="moe_gather",
    )(n_valid, slot_tok, slot_tok, h2)


def _moe_up_kernel(te_ref, nv_ref, x_ref, wg_ref, wu_ref, o_ref):
    @pl.when(pl.program_id(0) < nv_ref[0])
    def _():
        x = x_ref[...]
        a = jnp.dot(x, wg_ref[...], preferred_element_type=f32)
        u = jnp.dot(x, wu_ref[...], preferred_element_type=f32)
        o_ref[...] = (_silu(a) * u).astype(o_ref.dtype)

    @pl.when(pl.program_id(0) >= nv_ref[0])
    def _():
        o_ref[...] = jnp.zeros(o_ref.shape, o_ref.dtype)


def moe_up(tile_e, n_valid, xs, wg, wu, layer):
    P, D = xs.shape
    F = wg.shape[3]
    tm = MOE_TILE
    wspec = pl.BlockSpec((None, None, D, F), lambda t, te, nv: (layer, te[t], 0, 0), pipeline_mode=pl.Buffered(1))
    return pl.pallas_call(
        _moe_up_kernel,
        grid_spec=pltpu.PrefetchScalarGridSpec(
            num_scalar_prefetch=2, grid=(P // tm,),
            in_specs=[pl.BlockSpec((tm, D), lambda t, te, nv: (t, 0)), wspec, wspec],
            out_specs=pl.BlockSpec((tm, F), lambda t, te, nv: (t, 0))),
        out_shape=jax.ShapeDtypeStruct((P, F), bf16),
        compiler_params=_cparams("arbitrary"),
        name="moe_up",
    )(tile_e, n_valid, xs, wg, wu)


def _moe_down_kernel(te_ref, nv_ref, a_ref, w_ref, o_ref):
    @pl.when(pl.program_id(0) < nv_ref[0])
    def _():
        o_ref[...] = jnp.dot(a_ref[...], w_ref[...], preferred_element_type=f32)

    @pl.when(pl.program_id(0) >= nv_ref[0])
    def _():
        o_ref[...] = jnp.zeros(o_ref.shape, o_ref.dtype)


def moe_down(tile_e, n_valid, act, wd, layer):
    P, F = act.shape
    D = wd.shape[3]
    tm = MOE_TILE
    return pl.pallas_call(
        _moe_down_kernel,
        grid_spec=pltpu.PrefetchScalarGridSpec(
            num_scalar_prefetch=2, grid=(P // tm,),
            in_specs=[pl.BlockSpec((tm, F), lambda t, te, nv: (t, 0)),
                      pl.BlockSpec((None, None, F, D), lambda t, te, nv: (layer, te[t], 0, 0),
                                   pipeline_mode=pl.Buffered(1))],
            out_specs=pl.BlockSpec((tm, D), lambda t, te, nv: (t, 0))),
        out_shape=jax.ShapeDtypeStruct((P, D), f32),
        compiler_params=_cparams("arbitrary"),
        name="moe_down",
    )(tile_e, n_valid, act, wd)


def _moe_combine_kernel(pos_ref, pos_next_ref, w_ref, x_ref, g_ref, ys_hbm, o_ref, buf, sem):
    n = x_ref.shape[0]
    i = pl.program_id(0)
    slot = i % 2

    def issue(pos, s):
        def start(r, c):
            for kk in range(TOP_K):
                _row_copy(ys_hbm, buf.at[s, kk], sem.at[s], pos[0, kk, r], r).start()
            return c

        lax.fori_loop(0, n, start, 0)

    @pl.when(i == 0)
    def _():
        issue(pos_ref, 0)

    @pl.when(i + 1 < pl.num_programs(0))
    def _():
        issue(pos_next_ref, 1 - slot)

    def wait(r, c):
        for kk in range(TOP_K):
            _row_copy(ys_hbm, buf.at[slot, kk], sem.at[slot], 0, r).wait()
        return c

    lax.fori_loop(0, n, wait, 0)
    w = w_ref[...]
    y = w[:, 0:1] * buf[slot, 0]
    for kk in range(1, TOP_K):
        y = y + w[:, kk:kk + 1] * buf[slot, kk]
    o_ref[...] = x_ref[...] + g_ref[...] * y


def moe_combine(x, gate, ys, pos, top_w):
    M, D = x.shape
    n_tiles, _, tm = pos.shape
    return pl.pallas_call(
        _moe_combine_kernel,
        grid=(n_tiles,),
        in_specs=[pl.BlockSpec((1, TOP_K, tm), lambda i: (i, 0, 0), memory_space=pltpu.SMEM),
                  pl.BlockSpec((1, TOP_K, tm), lambda i: (jnp.minimum(i + 1, n_tiles - 1), 0, 0),
                               memory_space=pltpu.SMEM),
                  pl.BlockSpec((tm, TOP_K), lambda i: (i, 0)),
                  pl.BlockSpec((tm, D), lambda i: (i, 0)),
                  pl.BlockSpec((None, 1, D), lambda i: (_group_of_tile(i, tm), 0, 0)),
                  pl.BlockSpec(memory_space=pl.ANY)],
        out_specs=pl.BlockSpec((tm, D), lambda i: (i, 0)),
        out_shape=jax.ShapeDtypeStruct((M, D), f32),
        scratch_shapes=[pltpu.VMEM((2, TOP_K, tm, D), f32), pltpu.SemaphoreType.DMA((2,))],
        compiler_params=_cparams("arbitrary"),
        name="moe_combine",
    )(pos, pos, top_w, x, gate, ys)


def moe_ffn(x, norm_g, sc, sh, gate, w_router, wg, wu, wd, layer):
    M, D = x.shape
    E = w_router.shape[1]
    wr = jnp.zeros((D, LANE), f32).at[:, :E].set(w_router)
    h2, logits = norm_router(x, norm_g, sc, sh, wr)
    top_v, top_i = lax.top_k(logits[:, :E], TOP_K)
    top_w = jax.nn.softmax(top_v, axis=-1)
    nk = M * TOP_K
    n_tiles = -(-nk // MOE_TILE) + E
    n_slots = n_tiles * MOE_TILE
    e_flat = top_i.reshape(-1).astype(jnp.int32)
    onehot = (e_flat[:, None] == jnp.arange(E, dtype=jnp.int32)[None, :]).astype(jnp.int32)
    running = jnp.cumsum(onehot, axis=0)
    rank = jnp.sum((running - onehot) * onehot, axis=1)
    counts = running[-1]
    padded = (counts + MOE_TILE - 1) // MOE_TILE * MOE_TILE
    start = jnp.cumsum(counts) - counts
    pend = jnp.cumsum(padded)
    pstart = pend - padded
    pos = jnp.sum(onehot * pstart[None, :], axis=1) + rank
    n_valid = (pend[-1:] // MOE_TILE).astype(jnp.int32)
    tile_e = jnp.minimum(jnp.searchsorted(pend, jnp.arange(n_tiles, dtype=jnp.int32) * MOE_TILE, side='right'),
                         E - 1).astype(jnp.int32)
    order = jnp.argsort(e_flat, stable=True).astype(jnp.int32)
    slot = jnp.arange(n_slots, dtype=jnp.int32)
    slot_e = jnp.repeat(tile_e, MOE_TILE)
    r = slot - pstart[slot_e]
    live = r < counts[slot_e]
    slot_tok = jnp.where(live, order[jnp.where(live, start[slot_e] + r, 0)] // TOP_K, slot % M)
    xs = moe_gather(n_valid, h2, slot_tok.reshape(n_tiles, 1, MOE_TILE))
    act = moe_up(tile_e, n_valid, xs, wg, wu, layer)
    ys = moe_down(tile_e, n_valid, act, wd, layer)
    tc = _row_tile((256, 128))
    pos_t = pos.reshape(M // tc, tc, TOP_K).transpose(0, 2, 1)
    return moe_combine(x, gate, ys, pos_t, top_w)


def _rope_tables():
    rows = DEC_SEQ // GRID_W
    row = jnp.repeat(jnp.arange(rows), GRID_W)
    col = jnp.tile(jnp.arange(GRID_W), rows)
    quarter = HEAD_DIM // 4
    inv = ROPE_THETA ** (-jnp.arange(quarter, dtype=f32) / quarter)
    ar = row.astype(f32)[:, None] * inv[None, :]
    ac = col.astype(f32)[:, None] * inv[None, :]
    cos = jnp.concatenate([jnp.cos(ar), jnp.cos(ar), jnp.cos(ac), jnp.cos(ac)], axis=-1)
    sin = jnp.concatenate([-jnp.sin(ar), jnp.sin(ar), -jnp.sin(ac), jnp.sin(ac)], axis=-1)
    rep = LANE // HEAD_DIM
    return jnp.tile(cos, (1, rep)), jnp.tile(sin, (1, rep))


def kernel(x_prompt, x_sample, cache_k, cache_v, state_ssd, state_hgrn, c, c_ctx, w_ada, b_ada, norm1_g, norm2_g, w_in, attn_sink, ssd_conv_w, ssd_conv_b, ssd_dt_bias, ssd_a_log, ssd_d, ssd_norm_g, hgrn_lb, hgrn_norm_g, w_gate, w_br_att, w_br_ssd, w_br_hgrn, w_out, ffn_wg, ffn_wu, ffn_wd, router_w, moe_wg, moe_wu, moe_wd, final_g):
    att_w, kv_w, ssd_heads, conv_ch, hgrn_heads, nc, nl = _dims()
    D = D_MODEL
    G = 1 + DEC_BATCH
    g_pad = -(-G // SUBLANE) * SUBLANE
    x = jnp.concatenate([x_prompt.reshape(nc, D), x_sample.reshape(nl, D)], axis=0)
    m = jnp.zeros((g_pad, D), f32).at[0].set(c_ctx).at[1:G].set(c)
    mods = ada_mod(m, w_ada, b_ada)
    sm = jax.nn.softmax(hgrn_lb.astype(f32), axis=1)
    lb_all = jnp.cumsum(sm, axis=1) - sm[:, :1]
    cos, sin = _rope_tables()

    cuts = np.cumsum([0, att_w, kv_w, kv_w, SSD_W, conv_ch, 2 * ssd_heads, 5 * HGRN_W, GATE_RANK])
    bc_w = conv_ch - SSD_W
    off, o = {}, 0
    for name, w in (("hgrn", 5 * HGRN_W), ("q", att_w), ("z", SSD_W), ("xs", SSD_W), ("bc", bc_w),
                    ("kv", 2 * kv_w), ("gc", GATE_RANK), ("dt", LANE)):
        off[name] = o
        o += w

    def blk(name, width):
        assert off[name] % width == 0
        return off[name] // width

    ssd_cols = {"xs": blk("xs", SSD_W), "bc": blk("bc", bc_w), "z": blk("z", SSD_W), "dt": blk("dt", LANE)}
    expand = np.zeros((2, LANE, SSD_W), np.float32)
    for d in range(2):
        for hh in range(ssd_heads):
            expand[d, d * ssd_heads + hh, hh * SSD_HEAD_DIM:(hh + 1) * SSD_HEAD_DIM] = 1.0
    expand = jnp.asarray(expand)
    lanes = jnp.zeros((1, LANE), f32)
    zero_ssd = jnp.zeros((BATCH, ssd_heads * SSD_HEAD_DIM // LANE, SSD_STATE, LANE), f32)
    zero_h = jnp.zeros((BATCH, hgrn_heads, HGRN_DK, HGRN_DV), f32)

    wb = {"gate": w_gate, "br_att": w_br_att, "br_ssd": w_br_ssd, "br_hgrn": w_br_hgrn, "out": w_out,
          "ffn_g": ffn_wg, "ffn_u": ffn_wu, "ffn_d": ffn_wd, "moe_g": moe_wg, "moe_u": moe_wu, "moe_d": moe_wd}
    wb = {name: cast_bf16(w) for name, w in wb.items()}

    ks, vs, hs, ss = [], [], [], []
    for l in range(DEPTH):
        mod = mods[l].reshape(g_pad, 6, 1, D)
        sh1, sc1, g1, sh2, sc2, g2 = [mod[:, j] for j in range(6)]
        wl = w_in[l]
        w_proj = jnp.concatenate(
            [wl[:, cuts[6]:cuts[7]], wl[:, cuts[0]:cuts[1]], wl[:, cuts[3]:cuts[4]],
             wl[:, cuts[4]:cuts[4] + SSD_W], wl[:, cuts[4] + SSD_W:cuts[5]], wl[:, cuts[1]:cuts[3]],
             wl[:, cuts[7]:cuts[8]], wl[:, cuts[5]:cuts[6]], jnp.zeros((D, DT_PAD - 2 * ssd_heads), f32)],
            axis=1).astype(bf16)
        proj = norm_matmul(x, norm1_g[l], sc1, sh1, w_proj)

        y_att_c = attention_context(proj, blk("q", att_w), blk("kv", kv_w), attn_sink[l])
        q_rot, k_rot, vt = rope_latent(proj, blk("q", att_w), blk("kv", 2 * kv_w), cos, sin, nc)
        y_att_l = attention_latent(
            q_rot, k_rot, vt, cache_k[:, l].reshape(DEC_BATCH, PAST_LEN, kv_w).astype(bf16),
            cache_v[:, l].reshape(DEC_BATCH, PAST_LEN, kv_w).transpose(0, 2, 1).astype(bf16), attn_sink[l])
        y_att = (y_att_c, y_att_l)
        ks.append(proj[:nc, off["kv"]:off["kv"] + kv_w].reshape(BATCH, SEQ, N_KV, HEAD_DIM))
        vs.append(proj[:nc, off["kv"] + kv_w:off["kv"] + 2 * kv_w].reshape(BATCH, SEQ, N_KV, HEAD_DIM))

        prm = {
            "cw_x": ssd_conv_w[l][:, :SSD_W], "cb_x": ssd_conv_b[l][:SSD_W].reshape(1, SSD_W),
            "cw_bc": ssd_conv_w[l][:, SSD_W:], "cb_bc": ssd_conv_b[l][SSD_W:].reshape(1, bc_w),
            "dt_bias": lanes.at[0, :2 * ssd_heads].set(ssd_dt_bias[l].reshape(-1)),
            "a_lane": lanes.at[0, :2 * ssd_heads].set(-jnp.exp(ssd_a_log[l].astype(f32)).reshape(-1)),
            "expand": expand,
            "dskip": jnp.repeat(ssd_d[l].astype(f32), SSD_HEAD_DIM).reshape(1, SSD_W),
            "norm": ssd_norm_g[l].reshape(1, SSD_W),
        }
        y_ssd, h_ssd = [], []
        for (n_seq, T, row0, h0f, h0b) in (
                (BATCH, SEQ, 0, zero_ssd, zero_ssd),
                (DEC_BATCH, DEC_SEQ, nc, ssd_state_to_lanes(state_ssd[:, l, 0]), ssd_state_to_lanes(state_ssd[:, l, 1]))):
            yf, hf = ssd_pass(proj, ssd_cols, prm, h0f, n_seq, T, row0, False)
            yb, hb = ssd_pass(proj, ssd_cols, prm, h0b, n_seq, T, row0, True, y_prev=yf)
            y_ssd.append(yb)
            h_ssd.append((hf, hb))
        hs.append(jnp.stack([ssd_state_from_lanes(t) for t in h_ssd[0]], axis=1))

        assert off["hgrn"] == 0
        lb = lb_all[:, l]
        hn = hgrn_norm_g[l].reshape(1, HGRN_W)
        y_hgrn, s_hgrn = [], []
        for (n_seq, T, row0, s0f, s0b) in ((BATCH, SEQ, 0, zero_h, zero_h),
                                           (DEC_BATCH, DEC_SEQ, nc, state_hgrn[:, l, 0], state_hgrn[:, l, 1])):
            of, sf = hgrn_pass(proj, 1, lb[0:1], s0f, n_seq, T, row0, False)
            ob, sb = hgrn_pass(proj, 2, lb[1:2], s0b, n_seq, T, row0, True, o_prev=of, norm_g=hn)
            y_hgrn.append(ob)
            s_hgrn.append((sf, sb))
        ss.append(jnp.stack(s_hgrn[0], axis=1))

        merged = merge_branches(proj, blk("gc", GATE_RANK), y_att, y_ssd, y_hgrn, wb["gate"], wb["br_att"],
                                wb["br_ssd"], wb["br_hgrn"], l)
        x = matmul_residual(merged, wb["out"], l, x, g1, "out_proj")

        i = l // 2
        if l % 2 == 0:
            act = norm_glu_up(x, norm2_g[l], sc2, sh2, wb["ffn_g"], wb["ffn_u"], i)
            x = matmul_residual(act, wb["ffn_d"], i, x, g2, "ffn_down")
        else:
            x = moe_ffn(x, norm2_g[l], sc2, sh2, g2, router_w[i], wb["moe_g"], wb["moe_u"], wb["moe_d"], i)

    y_prompt = final_norm(x, final_g, 0, nc).reshape(BATCH, SEQ, D)
    y_sample = final_norm(x, final_g, nc, nl).reshape(DEC_BATCH, DEC_SEQ, D)
    return (y_prompt, y_sample, jnp.stack(ks, axis=1), jnp.stack(vs, axis=1),
            jnp.stack(hs, axis=1), jnp.stack(ss, axis=1))
```
